```python
import math
import jax, jax.numpy as jnp
from jax import lax
import numpy as np

D_MODEL = 1024
BATCH = 8
SEQ = 2048
DEPTH = 2
DEC_BATCH = 16
DEC_SEQ = 2048
PAST_LEN = 128

CHUNK = 128
A_GROUPS = 8
A_WIDTH = D_MODEL
A_GROUP_DIM = A_WIDTH // A_GROUPS
B_HEADS = 8
B_HEAD_DIM = D_MODEL // (2 * B_HEADS)
B_WIDTH = B_HEADS * 2 * B_HEAD_DIM
Q_BLOCK = 128
ATTN_SCALE = B_HEAD_DIM ** -0.5
N_BUCKETS = 32
MAX_DIST = 128
D_FF = 2816
CONV_W = 3
EPS = 1e-6
SPLITS = (A_WIDTH, 2 * A_WIDTH, 2 * A_WIDTH + B_WIDTH, 2 * A_WIDTH + 2 * B_WIDTH,
          2 * A_WIDTH + 3 * B_WIDTH, 2 * A_WIDTH + 3 * B_WIDTH + D_MODEL)
IN_COLS = 2 * A_WIDTH + 3 * B_WIDTH + 2 * D_MODEL

kernel_name = "hybrid_sgu_diffattn_encoder"


def rmsnorm(x, g):
    xf = x.astype(jnp.float32)
    y = xf * lax.rsqrt(jnp.mean(xf * xf, axis=-1, keepdims=True) + EPS) * g.astype(jnp.float32)
    return y.astype(x.dtype)


def rel_bucket(rel):
    nb = N_BUCKETS // 2
    max_exact = nb // 2
    ret = jnp.where(rel > 0, nb, 0)
    n = jnp.abs(rel)
    nf = jnp.maximum(n, 1).astype(jnp.float32)
    large = max_exact + (jnp.log(nf / max_exact) / math.log(MAX_DIST / max_exact)
                         * (nb - max_exact)).astype(jnp.int32)
    large = jnp.minimum(large, nb - 1)
    return ret + jnp.where(n < max_exact, n, large)


def spatial_gating(u, v, g_norm, w_s, b_s):
    bsz, s, _ = v.shape
    v = rmsnorm(v, g_norm)
    vc = v.reshape(bsz, s // CHUNK, CHUNK, A_GROUPS, A_GROUP_DIM)
    mixed = jnp.einsum('gts,bnsgc->bntgc', w_s, vc) + b_s.T[None, None, :, :, None]
    return u * mixed.reshape(bsz, s, A_WIDTH)


def diff_attention(q, k, v, lam, lam_init, bias_table, sub_g):
    bsz, s, _ = q.shape
    q = q.reshape(bsz, s, B_HEADS, 2, B_HEAD_DIM) * ATTN_SCALE
    k = k.reshape(bsz, s, B_HEADS, 2, B_HEAD_DIM)
    v = v.reshape(bsz, s, B_HEADS, 2 * B_HEAD_DIM)
    nblk = s // Q_BLOCK
    qb = q.reshape(bsz, nblk, Q_BLOCK, B_HEADS, 2, B_HEAD_DIM).transpose(1, 0, 2, 3, 4, 5)
    kpos = jnp.arange(s, dtype=jnp.int32)

    def block(args):
        qblk, i = args
        qpos = i * Q_BLOCK + jnp.arange(Q_BLOCK, dtype=jnp.int32)
        bias = bias_table[rel_bucket(kpos[None, :] - qpos[:, None])]
        bias = bias.transpose(2, 0, 1).astype(jnp.float32)
        logits = jnp.einsum('bqhmd,bkhmd->bmhqk', qblk, k).astype(jnp.float32) + bias[None, None]
        p = jax.nn.softmax(logits, axis=-1)
        attn = p[:, 0] - lam * p[:, 1]
        return jnp.einsum('bhqk,bkhd->bqhd', attn.astype(v.dtype), v)

    out = lax.map(block, (qb, jnp.arange(nblk, dtype=jnp.int32)))
    out = out.transpose(1, 0, 2, 3, 4).reshape(bsz, s, B_HEADS, 2 * B_HEAD_DIM)
    out = rmsnorm(out, sub_g) * (1.0 - lam_init)
    return out.reshape(bsz, s, B_WIDTH)


def conv_gated_mlp(h, w_up, conv_w, conv_b, w_down):
    s = h.shape[1]
    up = h @ w_up
    half = CONV_W // 2
    pad = jnp.pad(up, ((0, 0), (half, half), (0, 0)))
    conv = sum(pad[:, j:j + s] * conv_w[j] for j in range(CONV_W)) + conv_b
    a, b = jnp.split(conv, 2, axis=-1)
    return (jax.nn.gelu(a) * b) @ w_down


def trunk(x, rel_bias, g_mix, w_in, sgu_g, sgu_w, sgu_b, lam_qk, sub_g, w_br, w_out,
          g_ffn, w_up, conv_w, conv_b, w_down, g_final):
    for l in range(DEPTH):
        lam_init = 0.8 - 0.6 * math.exp(-0.3 * l)
        lq = lam_qk[l].astype(jnp.float32)
        lam = jnp.exp(jnp.sum(lq[0] * lq[1])) - jnp.exp(jnp.sum(lq[2] * lq[3])) + lam_init
        h = rmsnorm(x, g_mix[l])
        proj = h @ w_in[l]
        u_a, v_a, q, k, v_b, g_a, g_b = jnp.split(proj, SPLITS, axis=-1)
        y_a = spatial_gating(u_a, v_a, sgu_g[l], sgu_w[l], sgu_b[l])
        y_b = diff_attention(q, k, v_b, lam, lam_init, rel_bias, sub_g[l])
        br = jnp.einsum('nbsc,ncd->nbsd', jnp.stack([y_a, y_b], axis=0), w_br[l])
        merged = jax.nn.sigmoid(g_a) * br[0] + jax.nn.sigmoid(g_b) * br[1]
        x = x + merged @ w_out[l]
        x = x + conv_gated_mlp(rmsnorm(x, g_ffn[l]), w_up[l], conv_w[l], conv_b[l], w_down[l])
    return rmsnorm(x, g_final)


def setup_inputs(seed: int = 0) -> dict:
    key = jax.random.key(seed)
    ks = jax.random.split(key, 20)
    f32 = jnp.float32
    nrm = lambda k, shape, scale: jax.random.normal(k, shape, f32) * scale
    return {
        "x_prompt": nrm(ks[0], (BATCH, SEQ, D_MODEL), 1.0),
        "x_sample": nrm(ks[1], (DEC_BATCH, DEC_SEQ, D_MODEL), 1.0),
        "rel_bias": nrm(ks[2], (N_BUCKETS, B_HEADS), 0.5),
        "g_mix": 1.0 + nrm(ks[3], (DEPTH, D_MODEL), 0.02),
        "w_in": nrm(ks[4], (DEPTH, D_MODEL, IN_COLS), D_MODEL ** -0.5),
        "sgu_g": 1.0 + nrm(ks[5], (DEPTH, A_WIDTH), 0.02),
        "sgu_w": nrm(ks[6], (DEPTH, A_GROUPS, CHUNK, CHUNK), 0.5 * CHUNK ** -0.5),
        "sgu_b": 1.0 + nrm(ks[7], (DEPTH, A_GROUPS, CHUNK), 0.1),
        "lam_qk": nrm(ks[8], (DEPTH, 4, B_HEAD_DIM), 0.1),
        "sub_g": 1.0 + nrm(ks[9], (DEPTH, 2 * B_HEAD_DIM), 0.02),
        "w_br": nrm(ks[10], (DEPTH, 2, A_WIDTH, D_MODEL), A_WIDTH ** -0.5),
        "w_out": nrm(ks[11], (DEPTH, D_MODEL, D_MODEL), D_MODEL ** -0.5),
        "g_ffn": 1.0 + nrm(ks[12], (DEPTH, D_MODEL), 0.02),
        "w_up": nrm(ks[13], (DEPTH, D_MODEL, 2 * D_FF), D_MODEL ** -0.5),
        "conv_w": nrm(ks[14], (DEPTH, CONV_W, 2 * D_FF), CONV_W ** -0.5),
        "conv_b": nrm(ks[15], (DEPTH, 2 * D_FF), 0.02),
        "w_down": nrm(ks[16], (DEPTH, D_FF, D_MODEL), D_FF ** -0.5),
        "g_final": 1.0 + nrm(ks[17], (D_MODEL,), 0.02),
    }


def reference(x_prompt, x_sample, rel_bias, g_mix, w_in, sgu_g, sgu_w, sgu_b, lam_qk, sub_g,
              w_br, w_out, g_ffn, w_up, conv_w, conv_b, w_down, g_final):
    y_prompt = trunk(x_prompt, rel_bias, g_mix, w_in, sgu_g, sgu_w, sgu_b, lam_qk, sub_g, w_br,
                     w_out, g_ffn, w_up, conv_w, conv_b, w_down, g_final)
    y_sample = trunk(x_sample, rel_bias, g_mix, w_in, sgu_g, sgu_w, sgu_b, lam_qk, sub_g, w_br,
                     w_out, g_ffn, w_up, conv_w, conv_b, w_down, g_final)
    return (y_prompt, y_sample)
```

```python
import functools
import math

import jax
import jax.numpy as jnp
from jax import lax
from jax.experimental import pallas as pl
from jax.experimental.pallas import tpu as pltpu

D_MODEL = 1024
DEPTH = 2
CHUNK = 128
A_GROUPS = 8
A_WIDTH = D_MODEL
B_HEADS = 8
B_HEAD_DIM = D_MODEL // (2 * B_HEADS)
HEAD_W = 2 * B_HEAD_DIM
B_WIDTH = B_HEADS * HEAD_W
ATTN_SCALE = B_HEAD_DIM ** -0.5
N_BUCKETS = 32
MAX_DIST = 128
D_FF = 2816
CONV_W = 3
EPS = 1e-6
IN_COLS = 2 * A_WIDTH + 3 * B_WIDTH + 2 * D_MODEL

V7X_LANES = 128
V7X_SUBLANES = 8
V7X_MXU_DIM = 256
V7X_VMEM_BYTES = 64 * 1024 * 1024

TM_PROJ = 512
TM_MERGE = 512
TM_FFN = 512
HALO = V7X_SUBLANES
FC = V7X_MXU_DIM
N_FC = D_FF // FC
TQ = V7X_MXU_DIM
CK = V7X_MXU_DIM
N_BAND = 5
NEG_BIG = -1e30

assert D_FF % FC == 0
assert MAX_DIST <= CK and MAX_DIST <= TQ


def _vmem_limit(nbytes):
    return int(min(nbytes, V7X_VMEM_BYTES - 4 * 1024 * 1024))


def _rms(x, g):
    return x * lax.rsqrt(jnp.mean(x * x, axis=-1, keepdims=True) + EPS) * g


def _rel_bucket(rel):
    nb = N_BUCKETS // 2
    max_exact = nb // 2
    ret = jnp.where(rel > 0, nb, 0)
    n = jnp.abs(rel)
    nf = jnp.maximum(n, 1).astype(jnp.float32)
    large = max_exact + (jnp.log(nf / max_exact) / math.log(MAX_DIST / max_exact)
                         * (nb - max_exact)).astype(jnp.int32)
    large = jnp.minimum(large, nb - 1)
    return ret + jnp.where(n < max_exact, n, large)


def _band_kernel(table_ref, idx_ref, out_ref):
    h = pl.program_id(0)
    idx = idx_ref[...]
    acc = jnp.zeros(idx.shape, jnp.float32)
    for b in range(N_BUCKETS):
        acc = jnp.where(idx == b, table_ref[b, h], acc)
    out_ref[0] = acc


def _bias_band(rel_bias):
    d = jnp.arange(N_BAND, dtype=jnp.int32)[:, None, None] - N_BAND // 2
    kk = jnp.arange(CK, dtype=jnp.int32)[None, :, None]
    qq = jnp.arange(TQ, dtype=jnp.int32)[None, None, :]
    idx = _rel_bucket(d * CK + kk - qq)
    return pl.pallas_call(
        _band_kernel,
        out_shape=jax.ShapeDtypeStruct((B_HEADS, N_BAND, CK, TQ), jnp.float32),
        grid=(B_HEADS,),
        in_specs=[
            pl.BlockSpec(memory_space=pltpu.SMEM),
            pl.BlockSpec((N_BAND, CK, TQ), lambda h: (0, 0, 0)),
        ],
        out_specs=pl.BlockSpec((1, N_BAND, CK, TQ), lambda h: (h, 0, 0, 0)),
        name="bias_band",
    )(rel_bias, idx)


def _proj_kernel(x_ref, g_ref, w_ref, sg_ref, ws_ref, bs_ref,
                 ya_ref, q_ref, k_ref, v_ref, gate_ref):
    tm = x_ref.shape[0]
    h = _rms(x_ref[...], g_ref[...]).astype(jnp.bfloat16)

    def proj(lo, hi):
        return jnp.dot(h, w_ref[:, lo:hi], preferred_element_type=jnp.float32)

    u = proj(0, A_WIDTH)
    vn = _rms(proj(A_WIDTH, 2 * A_WIDTH), sg_ref[...]).astype(jnp.bfloat16)
    for n in range(tm // CHUNK):
        rows = slice(n * CHUNK, (n + 1) * CHUNK)
        for g in range(A_GROUPS):
            cols = slice(g * CHUNK, (g + 1) * CHUNK)
            mixed = jnp.dot(ws_ref[g], vn[rows, cols], preferred_element_type=jnp.float32)
            ya_ref[rows, cols] = (u[rows, cols] * (mixed + bs_ref[:, cols])).astype(ya_ref.dtype)

    base = 2 * A_WIDTH
    for i, o_ref in enumerate((q_ref, k_ref, v_ref)):
        r = proj(base + i * B_WIDTH, base + (i + 1) * B_WIDTH)
        for hd in range(B_HEADS):
            o_ref[0, hd] = r[:, hd * HEAD_W:(hd + 1) * HEAD_W].astype(o_ref.dtype)
    base += 3 * B_WIDTH
    gate_ref[...] = proj(base, base + 2 * D_MODEL).astype(gate_ref.dtype)


def _proj_call(x, g_mix, w_in, sgu_g, sgu_w, sgu_bfull):
    bsz, s, _ = x.shape
    m = bsz * s
    tm = TM_PROJ
    tiles_per_seq = s // tm
    const = lambda shape: pl.BlockSpec(shape, lambda i: (0,) * len(shape),
                                       pipeline_mode=pl.Buffered(1))
    head_spec = pl.BlockSpec((1, B_HEADS, tm, HEAD_W),
                             lambda i: (i // tiles_per_seq, 0, i % tiles_per_seq, 0))
    head_shape = jax.ShapeDtypeStruct((bsz, B_HEADS, s, HEAD_W), jnp.bfloat16)
    return pl.pallas_call(
        _proj_kernel,
        out_shape=(jax.ShapeDtypeStruct((m, A_WIDTH), jnp.bfloat16),
                   head_shape, head_shape, head_shape,
                   jax.ShapeDtypeStruct((m, 2 * D_MODEL), jnp.bfloat16)),
        grid=(m // tm,),
        in_specs=[
            pl.BlockSpec((tm, D_MODEL), lambda i: (i, 0)),
            const((1, D_MODEL)),
            const((D_MODEL, IN_COLS)),
            const((1, A_WIDTH)),
            const((A_GROUPS, CHUNK, CHUNK)),
            const((CHUNK, A_WIDTH)),
        ],
        out_specs=(pl.BlockSpec((tm, A_WIDTH), lambda i: (i, 0)),
                   head_spec, head_spec, head_spec,
                   pl.BlockSpec((tm, 2 * D_MODEL), lambda i: (i, 0))),
        compiler_params=pltpu.CompilerParams(
            dimension_semantics=("parallel",),
            vmem_limit_bytes=_vmem_limit(56 * 1024 * 1024)),
        name="proj_sgu",
    )(x.reshape(m, D_MODEL), g_mix, w_in, sgu_g, sgu_w, sgu_bfull)


def _attn_kernel(lam_init, q_ref, k_ref, v_ref, band_ref, lq_ref, subg_ref, o_ref,
                 vt_ref, acc_ref):
    s = q_ref.shape[2]
    nq, nk = s // TQ, s // CK

    lq = lq_ref[...]
    lam = (jnp.exp(jnp.sum(lq[0:1] * lq[1:2], axis=-1, keepdims=True))
           - jnp.exp(jnp.sum(lq[2:3] * lq[3:4], axis=-1, keepdims=True)) + lam_init)

    for c in range(nk):
        vt_ref[c] = v_ref[0, 0, c * CK:(c + 1) * CK, :].astype(jnp.float32).T.astype(vt_ref.dtype)

    lane = lax.broadcasted_iota(jnp.int32, (TQ, HEAD_W), 1)
    nt_dims = (((1,), (1,)), ((), ()))

    def q_body(qi, carry):
        q = q_ref[0, 0, pl.ds(pl.multiple_of(qi * TQ, TQ), TQ), :].astype(jnp.float32)
        qm = (jnp.where(lane < B_HEAD_DIM, q, 0.0).astype(jnp.bfloat16),
              jnp.where(lane >= B_HEAD_DIM, q, 0.0).astype(jnp.bfloat16))
        acc_ref[...] = jnp.zeros(acc_ref.shape, acc_ref.dtype)

        def k_body(kc, stats):
            k = k_ref[0, 0, pl.ds(pl.multiple_of(kc * CK, CK), CK), :]
            d = jnp.clip(kc - qi, -(N_BAND // 2), N_BAND // 2) + N_BAND // 2
            bias = band_ref[0, d]
            vt = vt_ref[kc]
            new = []
            for m in range(2):
                m_old, s_old = stats[2 * m], stats[2 * m + 1]
                logit = lax.dot_general(k, qm[m], nt_dims,
                                        preferred_element_type=jnp.float32) + bias
                m_new = jnp.maximum(m_old, jnp.max(logit, axis=0, keepdims=True))
                alpha = jnp.exp(m_old - m_new)
                p = jnp.exp(logit - m_new)
                s_new = s_old * alpha + jnp.sum(p, axis=0, keepdims=True)
                acc_ref[m] = acc_ref[m] * alpha + jnp.dot(
                    vt, p.astype(jnp.bfloat16), preferred_element_type=jnp.float32)
                new += [m_new, s_new]
            return tuple(new)

        init = (jnp.full((1, TQ), NEG_BIG, jnp.float32), jnp.zeros((1, TQ), jnp.float32)) * 2
        _, s0, _, s1 = lax.fori_loop(0, nk, k_body, init)
        o = acc_ref[0] / s0 - acc_ref[1] * (lam / s1)
        o = o * lax.rsqrt(jnp.mean(o * o, axis=0, keepdims=True) + EPS)
        o = o.T * subg_ref[...] * (1.0 - lam_init)
        o_ref[0, 0, pl.ds(pl.multiple_of(qi * TQ, TQ), TQ), :] = o.astype(o_ref.dtype)
        return carry

    lax.fori_loop(0, nq, q_body, 0)


def _attn_call(q, k, v, band, lam_qk, sub_g, lam_init):
    bsz, _, s, _ = q.shape
    head = pl.BlockSpec((1, 1, s, HEAD_W), lambda b, h: (b, h, 0, 0))
    return pl.pallas_call(
        functools.partial(_attn_kernel, lam_init),
        out_shape=jax.ShapeDtypeStruct(q.shape, jnp.bfloat16),
        grid=(bsz, B_HEADS),
        in_specs=[
            head, head, head,
            pl.BlockSpec((1, N_BAND, CK, TQ), lambda b, h: (h, 0, 0, 0)),
            pl.BlockSpec((4, B_HEAD_DIM), lambda b, h: (0, 0)),
            pl.BlockSpec((1, HEAD_W), lambda b, h: (0, 0)),
        ],
        out_specs=head,
        scratch_shapes=[
            pltpu.VMEM((s // CK, HEAD_W, CK), jnp.bfloat16),
            pltpu.VMEM((2, HEAD_W, TQ), jnp.float32),
        ],
        compiler_params=pltpu.CompilerParams(
            dimension_semantics=("parallel", "parallel"),
            vmem_limit_bytes=_vmem_limit(32 * 1024 * 1024)),
        name="diff_attn",
    )(q, k, v, band, lam_qk, sub_g)


def _merge_kernel(x_ref, ya_ref, yb_ref, gate_ref, wa_ref, wb_ref, wo_ref, o_ref):
    yb = jnp.concatenate([yb_ref[0, hd] for hd in range(B_HEADS)], axis=-1)
    br_a = jnp.dot(ya_ref[...], wa_ref[...], preferred_element_type=jnp.float32)
    br_b = jnp.dot(yb, wb_ref[...], preferred_element_type=jnp.float32)
    gate = gate_ref[...].astype(jnp.float32)
    merged = (jax.nn.sigmoid(gate[:, :D_MODEL]) * br_a
              + jax.nn.sigmoid(gate[:, D_MODEL:]) * br_b)
    o_ref[...] = x_ref[...] + jnp.dot(merged.astype(jnp.bfloat16), wo_ref[...],
                                      preferred_element_type=jnp.float32)


def _merge_call(x, ya, yb, gates, w_br_a, w_br_b, w_out):
    bsz, s, _ = x.shape
    m = bsz * s
    tm = TM_MERGE
    tiles_per_seq = s // tm
    row = lambda width: pl.BlockSpec((tm, width), lambda i: (i, 0))
    const = lambda shape: pl.BlockSpec(shape, lambda i: (0,) * len(shape),
                                       pipeline_mode=pl.Buffered(1))
    out = pl.pallas_call(
        _merge_kernel,
        out_shape=jax.ShapeDtypeStruct((m, D_MODEL), jnp.float32),
        grid=(m // tm,),
        in_specs=[
            row(D_MODEL), row(A_WIDTH),
            pl.BlockSpec((1, B_HEADS, tm, HEAD_W),
                         lambda i: (i // tiles_per_seq, 0, i % tiles_per_seq, 0)),
            row(2 * D_MODEL),
            const((A_WIDTH, D_MODEL)), const((B_WIDTH, D_MODEL)), const((D_MODEL, D_MODEL)),
        ],
        out_specs=row(D_MODEL),
        compiler_params=pltpu.CompilerParams(
            dimension_semantics=("parallel",),
            vmem_limit_bytes=_vmem_limit(48 * 1024 * 1024)),
        name="merge_out",
    )(x.reshape(m, D_MODEL), ya, yb, gates, w_br_a, w_br_b, w_out)
    return out.reshape(bsz, s, D_MODEL)


def _ffn_kernel(final, x_ref, xp_ref, xn_ref, g_ref, wup_ref, cw_ref, cb_ref, wdn_ref, gf_ref,
                o_ref, acc_ref):
    tm = x_ref.shape[1]
    j = pl.program_id(1)
    x = x_ref[0]
    keep_p = (j > 0).astype(jnp.float32)
    keep_n = (j < pl.num_programs(1) - 1).astype(jnp.float32)
    g = g_ref[...]
    hext = jnp.concatenate([_rms(xp_ref[0], g) * keep_p, _rms(x, g), _rms(xn_ref[0], g) * keep_n],
                           axis=0).astype(jnp.bfloat16)
    ext = tm + 2 * HALO
    acc_ref[...] = jnp.zeros(acc_ref.shape, acc_ref.dtype)

    def conv(c):
        up = jnp.dot(hext, wup_ref[c], preferred_element_type=jnp.float32)
        prev = pltpu.roll(up, 1, 0)
        nxt = pltpu.roll(up, ext - 1, 0)
        w = cw_ref[c]
        y = prev * w[0:1] + up * w[1:2] + nxt * w[2:3] + cb_ref[c]
        return y[HALO:HALO + tm]

    def body(c, carry):
        a = conv(c)
        b = conv(N_FC + c)
        gated = (jax.nn.gelu(a) * b).astype(jnp.bfloat16)
        acc_ref[...] += jnp.dot(gated, wdn_ref[c], preferred_element_type=jnp.float32)
        return carry

    lax.fori_loop(0, N_FC, body, 0)
    y = x + acc_ref[...]
    if final:
        y = _rms(y, gf_ref[...])
    o_ref[0] = y


def _ffn_call(x, g_ffn, w_up, conv_w, conv_b, w_down, g_final, final):
    bsz, s, _ = x.shape
    tm = TM_FFN
    nt = s // tm
    hb = tm // HALO
    const = lambda shape: pl.BlockSpec(shape, lambda b, j: (0,) * len(shape),
                                       pipeline_mode=pl.Buffered(1))
    return pl.pallas_call(
        functools.partial(_ffn_kernel, final),
        out_shape=jax.ShapeDtypeStruct(x.shape, jnp.float32),
        grid=(bsz, nt),
        in_specs=[
            pl.BlockSpec((1, tm, D_MODEL), lambda b, j: (b, j, 0)),
            pl.BlockSpec((1, HALO, D_MODEL), lambda b, j: (b, jnp.maximum(j * hb - 1, 0), 0)),
            pl.BlockSpec((1, HALO, D_MODEL),
                         lambda b, j: (b, jnp.minimum((j + 1) * hb, nt * hb - 1), 0)),
            const((1, D_MODEL)),
            const((2 * N_FC, D_MODEL, FC)),
            const((2 * N_FC, CONV_W, FC)),
            const((2 * N_FC, 1, FC)),
            const((N_FC, FC, D_MODEL)),
            const((1, D_MODEL)),
        ],
        out_specs=pl.BlockSpec((1, tm, D_MODEL), lambda b, j: (b, j, 0)),
        scratch_shapes=[pltpu.VMEM((tm, D_MODEL), jnp.float32)],
        compiler_params=pltpu.CompilerParams(
            dimension_semantics=("parallel", "parallel"),
            vmem_limit_bytes=_vmem_limit(56 * 1024 * 1024)),
        name="conv_ffn",
    )(x, x, x, g_ffn, w_up, conv_w, conv_b, w_down, g_final)


def _prep_layer(l, g_mix, w_in, sgu_g, sgu_w, sgu_b, lam_qk, sub_g, w_br, w_out,
                g_ffn, w_up, conv_w, conv_b, w_down):
    bf = jnp.bfloat16
    q_lo, q_hi = 2 * A_WIDTH, 2 * A_WIDTH + B_WIDTH
    col_scale = jnp.ones((IN_COLS,), jnp.float32).at[q_lo:q_hi].set(ATTN_SCALE)
    chunked = lambda a: a.reshape(a.shape[0], 2 * N_FC, FC).transpose(1, 0, 2)
    return dict(
        g_mix=g_mix[l][None],
        w_in=(w_in[l] * col_scale).astype(bf),
        sgu_g=sgu_g[l][None],
        sgu_w=sgu_w[l].astype(bf),
        sgu_bfull=jnp.repeat(sgu_b[l].T, CHUNK, axis=1),
        lam_qk=lam_qk[l],
        sub_g=sub_g[l][None],
        w_br_a=w_br[l, 0].astype(bf),
        w_br_b=w_br[l, 1].astype(bf),
        w_out=w_out[l].astype(bf),
        g_ffn=g_ffn[l][None],
        w_up=chunked(w_up[l]).astype(bf),
        conv_w=chunked(conv_w[l]),
        conv_b=chunked(conv_b[l][None]),
        w_down=w_down[l].reshape(N_FC, FC, D_MODEL).astype(bf),
    )


def _trunk(x, band, layers, g_final):
    for l, p in enumerate(layers):
        lam_init = 0.8 - 0.6 * math.exp(-0.3 * l)
        ya, q, k, v, gates = _proj_call(x, p["g_mix"], p["w_in"], p["sgu_g"], p["sgu_w"],
                                        p["sgu_bfull"])
        yb = _attn_call(q, k, v, band, p["lam_qk"], p["sub_g"], lam_init)
        x = _merge_call(x, ya, yb, gates, p["w_br_a"], p["w_br_b"], p["w_out"])
        x = _ffn_call(x, p["g_ffn"], p["w_up"], p["conv_w"], p["conv_b"], p["w_down"],
                      g_final, final=(l == len(layers) - 1))
    return x


def kernel(x_prompt, x_sample, rel_bias, g_mix, w_in, sgu_g, sgu_w, sgu_b, lam_qk, sub_g,
           w_br, w_out, g_ffn, w_up, conv_w, conv_b, w_down, g_final):
    band = _bias_band(rel_bias)
    layers = [_prep_layer(l, g_mix, w_in, sgu_g, sgu_w, sgu_b, lam_qk, sub_g, w_br, w_out,
                          g_ffn, w_up, conv_w, conv_b, w_down) for l in range(DEPTH)]
    g_fin = g_final[None]
    return (_trunk(x_prompt, band, layers, g_fin), _trunk(x_sample, band, layers, g_fin))
```

```python
import functools
import math

import jax
import jax.numpy as jnp
from jax import lax
from jax.experimental import pallas as pl
from jax.experimental.pallas import tpu as pltpu

D_MODEL = 1024
DEPTH = 2
CHUNK = 128
A_GROUPS = 8
A_WIDTH = D_MODEL
B_HEADS = 8
B_HEAD_DIM = D_MODEL // (2 * B_HEADS)
HEAD_W = 2 * B_HEAD_DIM
B_WIDTH = B_HEADS * HEAD_W
ATTN_SCALE = B_HEAD_DIM ** -0.5
N_BUCKETS = 32
MAX_DIST = 128
D_FF = 2816
CONV_W = 3
EPS = 1e-6
IN_COLS = 2 * A_WIDTH + 3 * B_WIDTH + 2 * D_MODEL

V7X_LANES = 128
V7X_SUBLANES = 8
V7X_MXU_DIM = 256
V7X_VMEM_BYTES = 64 * 1024 * 1024

TM_PROJ = 512
TM_MERGE = 512
TM_FFN = 512
HALO = V7X_SUBLANES
FC = V7X_MXU_DIM
N_FC = D_FF // FC
TQ = V7X_MXU_DIM
CK = V7X_MXU_DIM
N_BAND = 5
NEG_BIG = -1e30

assert D_FF % FC == 0
assert MAX_DIST <= CK and MAX_DIST <= TQ


def _vmem_limit(nbytes):
    return int(min(nbytes, V7X_VMEM_BYTES - 4 * 1024 * 1024))


def _rms(x, g):
    return x * lax.rsqrt(jnp.mean(x * x, axis=-1, keepdims=True) + EPS) * g


def _rel_bucket(rel):
    nb = N_BUCKETS // 2
    max_exact = nb // 2
    ret = jnp.where(rel > 0, nb, 0)
    n = jnp.abs(rel)
    nf = jnp.maximum(n, 1).astype(jnp.float32)
    large = max_exact + (jnp.log(nf / max_exact) / math.log(MAX_DIST / max_exact)
                         * (nb - max_exact)).astype(jnp.int32)
    large = jnp.minimum(large, nb - 1)
    return ret + jnp.where(n < max_exact, n, large)


def _band_kernel(table_ref, idx_ref, out_ref):
    h = pl.program_id(0)
    idx = idx_ref[...]
    acc = jnp.zeros(idx.shape, jnp.float32)
    for b in range(N_BUCKETS):
        acc = jnp.where(idx == b, table_ref[b, h], acc)
    out_ref[0] = acc


def _bias_band(rel_bias):
    d = jnp.arange(N_BAND, dtype=jnp.int32)[:, None, None] - N_BAND // 2
    kk = jnp.arange(CK, dtype=jnp.int32)[None, :, None]
    qq = jnp.arange(TQ, dtype=jnp.int32)[None, None, :]
    idx = _rel_bucket(d * CK + kk - qq)
    return pl.pallas_call(
        _band_kernel,
        out_shape=jax.ShapeDtypeStruct((B_HEADS, N_BAND, CK, TQ), jnp.float32),
        grid=(B_HEADS,),
        in_specs=[
            pl.BlockSpec(memory_space=pltpu.SMEM),
            pl.BlockSpec((N_BAND, CK, TQ), lambda h: (0, 0, 0)),
        ],
        out_specs=pl.BlockSpec((1, N_BAND, CK, TQ), lambda h: (h, 0, 0, 0)),
        name="bias_band",
    )(rel_bias, idx)


def _proj_kernel(x_ref, g_ref, w_ref, sg_ref, ws_ref, bs_ref,
                 ya_ref, q_ref, k_ref, v_ref, gate_ref):
    tm = x_ref.shape[0]
    h = _rms(x_ref[...], g_ref[...]).astype(jnp.bfloat16)

    def proj(lo, hi):
        return jnp.dot(h, w_ref[:, lo:hi], preferred_element_type=jnp.float32)

    u = proj(0, A_WIDTH)
    vn = _rms(proj(A_WIDTH, 2 * A_WIDTH), sg_ref[...]).astype(jnp.bfloat16)
    for n in range(tm // CHUNK):
        rows = slice(n * CHUNK, (n + 1) * CHUNK)
        for g in range(A_GROUPS):
            cols = slice(g * CHUNK, (g + 1) * CHUNK)
            mixed = jnp.dot(ws_ref[g], vn[rows, cols], preferred_element_type=jnp.float32)
            ya_ref[rows, cols] = (u[rows, cols] * (mixed + bs_ref[:, cols])).astype(ya_ref.dtype)

    base = 2 * A_WIDTH
    for i, o_ref in enumerate((q_ref, k_ref, v_ref)):
        r = proj(base + i * B_WIDTH, base + (i + 1) * B_WIDTH)
        for hd in range(B_HEADS):
            o_ref[0, hd] = r[:, hd * HEAD_W:(hd + 1) * HEAD_W].astype(o_ref.dtype)
    base += 3 * B_WIDTH
    gate_ref[...] = proj(base, base + 2 * D_MODEL).astype(gate_ref.dtype)


def _proj_call(x, g_mix, w_in, sgu_g, sgu_w, sgu_bfull):
    bsz, s, _ = x.shape
    m = bsz * s
    tm = TM_PROJ
    tiles_per_seq = s // tm
    const = lambda shape: pl.BlockSpec(shape, lambda i: (0,) * len(shape),
                                       pipeline_mode=pl.Buffered(1))
    head_spec = pl.BlockSpec((1, B_HEADS, tm, HEAD_W),
                             lambda i: (i // tiles_per_seq, 0, i % tiles_per_seq, 0))
    head_shape = jax.ShapeDtypeStruct((bsz, B_HEADS, s, HEAD_W), jnp.bfloat16)
    return pl.pallas_call(
        _proj_kernel,
        out_shape=(jax.ShapeDtypeStruct((m, A_WIDTH), jnp.bfloat16),
                   head_shape, head_shape, head_shape,
                   jax.ShapeDtypeStruct((m, 2 * D_MODEL), jnp.bfloat16)),
        grid=(m // tm,),
        in_specs=[
            pl.BlockSpec((tm, D_MODEL), lambda i: (i, 0)),
            const((1, D_MODEL)),
            const((D_MODEL, IN_COLS)),
            const((1, A_WIDTH)),
            const((A_GROUPS, CHUNK, CHUNK)),
            const((CHUNK, A_WIDTH)),
        ],
        out_specs=(pl.BlockSpec((tm, A_WIDTH), lambda i: (i, 0)),
                   head_spec, head_spec, head_spec,
                   pl.BlockSpec((tm, 2 * D_MODEL), lambda i: (i, 0))),
        compiler_params=pltpu.CompilerParams(
            dimension_semantics=("parallel",),
            vmem_limit_bytes=_vmem_limit(56 * 1024 * 1024)),
        name="proj_sgu",
    )(x.reshape(m, D_MODEL), g_mix, w_in, sgu_g, sgu_w, sgu_bfull)


def _attn_kernel(lam_init, q_ref, k_ref, v_ref, band_ref, lq_ref, subg_ref, o_ref,
                 vt_ref, l_ref, p_ref):
    s = q_ref.shape[2]
    nq, nk = s // TQ, s // CK
    rg = CK // V7X_SUBLANES

    lq = lq_ref[...]
    lam = (jnp.exp(jnp.sum(lq[0:1] * lq[1:2], axis=-1, keepdims=True))
           - jnp.exp(jnp.sum(lq[2:3] * lq[3:4], axis=-1, keepdims=True)) + lam_init)

    for c in range(nk):
        cols = slice(c * CK, (c + 1) * CK)
        vt_ref[:, cols] = v_ref[0, 0, cols, :].astype(jnp.float32).T.astype(vt_ref.dtype)

    lane = lax.broadcasted_iota(jnp.int32, (TQ, HEAD_W), 1)
    nt_dims = (((1,), (1,)), ((), ()))

    def q_body(qi, carry):
        q = q_ref[0, 0, pl.ds(pl.multiple_of(qi * TQ, TQ), TQ), :].astype(jnp.float32)
        qm = (jnp.where(lane < B_HEAD_DIM, q, 0.0).astype(jnp.bfloat16),
              jnp.where(lane >= B_HEAD_DIM, q, 0.0).astype(jnp.bfloat16))

        mrow = []
        for m in range(2):
            logit = lax.dot_general(k_ref[0, 0], qm[m], nt_dims,
                                    preferred_element_type=jnp.float32)
            mx = None
            for c in range(nk):
                d = jnp.clip(c - qi, -(N_BAND // 2), N_BAND // 2) + N_BAND // 2
                lc = logit[c * CK:(c + 1) * CK] + band_ref[0, d]
                l_ref[m, c] = lc
                cm = jnp.max(lc.reshape(rg, V7X_SUBLANES, TQ), axis=0)
                mx = cm if mx is None else jnp.maximum(mx, cm)
            mrow.append(jnp.max(mx, axis=0, keepdims=True))

        outs = []
        for m in range(2):
            sm = jnp.zeros((V7X_SUBLANES, TQ), jnp.float32)
            for c in range(nk):
                p = jnp.exp(l_ref[m, c] - mrow[m])
                sm = sm + jnp.sum(p.reshape(rg, V7X_SUBLANES, TQ), axis=0)
                p_ref[m, c * CK:(c + 1) * CK, :] = p.astype(p_ref.dtype)
            acc = jnp.dot(vt_ref[...], p_ref[m], preferred_element_type=jnp.float32)
            outs.append((acc, jnp.sum(sm, axis=0, keepdims=True)))

        (acc0, s0), (acc1, s1) = outs
        o = acc0 / s0 - acc1 * (lam / s1)
        o = o * lax.rsqrt(jnp.mean(o * o, axis=0, keepdims=True) + EPS)
        o = o.T * subg_ref[...] * (1.0 - lam_init)
        o_ref[0, 0, pl.ds(pl.multiple_of(qi * TQ, TQ), TQ), :] = o.astype(o_ref.dtype)
        return carry

    lax.fori_loop(0, nq, q_body, 0)


def _attn_call(q, k, v, band, lam_qk, sub_g, lam_init):
    bsz, _, s, _ = q.shape
    head = pl.BlockSpec((1, 1, s, HEAD_W), lambda b, h: (b, h, 0, 0))
    return pl.pallas_call(
        functools.partial(_attn_kernel, lam_init),
        out_shape=jax.ShapeDtypeStruct(q.shape, jnp.bfloat16),
        grid=(bsz, B_HEADS),
        in_specs=[
            head, head, head,
            pl.BlockSpec((1, N_BAND, CK, TQ), lambda b, h: (h, 0, 0, 0)),
            pl.BlockSpec((4, B_HEAD_DIM), lambda b, h: (0, 0)),
            pl.BlockSpec((1, HEAD_W), lambda b, h: (0, 0)),
        ],
        out_specs=head,
        scratch_shapes=[
            pltpu.VMEM((HEAD_W, s), jnp.bfloat16),
            pltpu.VMEM((2, s // CK, CK, TQ), jnp.float32),
            pltpu.VMEM((2, s, TQ), jnp.bfloat16),
        ],
        compiler_params=pltpu.CompilerParams(
            dimension_semantics=("parallel", "parallel"),
            vmem_limit_bytes=_vmem_limit(32 * 1024 * 1024)),
        name="diff_attn",
    )(q, k, v, band, lam_qk, sub_g)


def _merge_kernel(x_ref, ya_ref, yb_ref, gate_ref, wa_ref, wb_ref, wo_ref, o_ref):
    yb = jnp.concatenate([yb_ref[0, hd] for hd in range(B_HEADS)], axis=-1)
    br_a = jnp.dot(ya_ref[...], wa_ref[...], preferred_element_type=jnp.float32)
    br_b = jnp.dot(yb, wb_ref[...], preferred_element_type=jnp.float32)
    gate = gate_ref[...].astype(jnp.float32)
    merged = (jax.nn.sigmoid(gate[:, :D_MODEL]) * br_a
              + jax.nn.sigmoid(gate[:, D_MODEL:]) * br_b)
    o_ref[...] = x_ref[...] + jnp.dot(merged.astype(jnp.bfloat16), wo_ref[...],
                                      preferred_element_type=jnp.float32)


def _merge_call(x, ya, yb, gates, w_br_a, w_br_b, w_out):
    bsz, s, _ = x.shape
    m = bsz * s
    tm = TM_MERGE
    tiles_per_seq = s // tm
    row = lambda width: pl.BlockSpec((tm, width), lambda i: (i, 0))
    const = lambda shape: pl.BlockSpec(shape, lambda i: (0,) * len(shape),
                                       pipeline_mode=pl.Buffered(1))
    out = pl.pallas_call(
        _merge_kernel,
        out_shape=jax.ShapeDtypeStruct((m, D_MODEL), jnp.float32),
        grid=(m // tm,),
        in_specs=[
            row(D_MODEL), row(A_WIDTH),
            pl.BlockSpec((1, B_HEADS, tm, HEAD_W),
                         lambda i: (i // tiles_per_seq, 0, i % tiles_per_seq, 0)),
            row(2 * D_MODEL),
            const((A_WIDTH, D_MODEL)), const((B_WIDTH, D_MODEL)), const((D_MODEL, D_MODEL)),
        ],
        out_specs=row(D_MODEL),
        compiler_params=pltpu.CompilerParams(
            dimension_semantics=("parallel",),
            vmem_limit_bytes=_vmem_limit(48 * 1024 * 1024)),
        name="merge_out",
    )(x.reshape(m, D_MODEL), ya, yb, gates, w_br_a, w_br_b, w_out)
    return out.reshape(bsz, s, D_MODEL)


def _ffn_kernel(final, x_ref, xp_ref, xn_ref, g_ref, wup_ref, cw_ref, cb_ref, wdn_ref, gf_ref,
                o_ref, acc_ref):
    tm = x_ref.shape[1]
    j = pl.program_id(1)
    x = x_ref[0]
    keep_p = (j > 0).astype(jnp.float32)
    keep_n = (j < pl.num_programs(1) - 1).astype(jnp.float32)
    g = g_ref[...]
    hext = jnp.concatenate([_rms(xp_ref[0], g) * keep_p, _rms(x, g), _rms(xn_ref[0], g) * keep_n],
                           axis=0).astype(jnp.bfloat16)
    ext = tm + 2 * HALO
    acc_ref[...] = jnp.zeros(acc_ref.shape, acc_ref.dtype)

    def conv(c):
        up = jnp.dot(hext, wup_ref[c], preferred_element_type=jnp.float32)
        prev = pltpu.roll(up, 1, 0)
        nxt = pltpu.roll(up, ext - 1, 0)
        w = cw_ref[c]
        y = prev * w[0:1] + up * w[1:2] + nxt * w[2:3] + cb_ref[c]
        return y[HALO:HALO + tm]

    def body(c, carry):
        a = conv(c)
        b = conv(N_FC + c)
        gated = (jax.nn.gelu(a) * b).astype(jnp.bfloat16)
        acc_ref[...] += jnp.dot(gated, wdn_ref[c], preferred_element_type=jnp.float32)
        return carry

    lax.fori_loop(0, N_FC, body, 0)
    y = x + acc_ref[...]
    if final:
        y = _rms(y, gf_ref[...])
    o_ref[0] = y


def _ffn_call(x, g_ffn, w_up, conv_w, conv_b, w_down, g_final, final):
    bsz, s, _ = x.shape
    tm = TM_FFN
    nt = s // tm
    hb = tm // HALO
    const = lambda shape: pl.BlockSpec(shape, lambda b, j: (0,) * len(shape),
                                       pipeline_mode=pl.Buffered(1))
    return pl.pallas_call(
        functools.partial(_ffn_kernel, final),
        out_shape=jax.ShapeDtypeStruct(x.shape, jnp.float32),
        grid=(bsz, nt),
        in_specs=[
            pl.BlockSpec((1, tm, D_MODEL), lambda b, j: (b, j, 0)),
            pl.BlockSpec((1, HALO, D_MODEL), lambda b, j: (b, jnp.maximum(j * hb - 1, 0), 0)),
            pl.BlockSpec((1, HALO, D_MODEL),
                         lambda b, j: (b, jnp.minimum((j + 1) * hb, nt * hb - 1), 0)),
            const((1, D_MODEL)),
            const((2 * N_FC, D_MODEL, FC)),
            const((2 * N_FC, CONV_W, FC)),
            const((2 * N_FC, 1, FC)),
            const((N_FC, FC, D_MODEL)),
            const((1, D_MODEL)),
        ],
        out_specs=pl.BlockSpec((1, tm, D_MODEL), lambda b, j: (b, j, 0)),
        scratch_shapes=[pltpu.VMEM((tm, D_MODEL), jnp.float32)],
        compiler_params=pltpu.CompilerParams(
            dimension_semantics=("parallel", "parallel"),
            vmem_limit_bytes=_vmem_limit(56 * 1024 * 1024)),
        name="conv_ffn",
    )(x, x, x, g_ffn, w_up, conv_w, conv_b, w_down, g_final)


def _prep_layer(l, g_mix, w_in, sgu_g, sgu_w, sgu_b, lam_qk, sub_g, w_br, w_out,
                g_ffn, w_up, conv_w, conv_b, w_down):
    bf = jnp.bfloat16
    q_lo, q_hi = 2 * A_WIDTH, 2 * A_WIDTH + B_WIDTH
    col_scale = jnp.ones((IN_COLS,), jnp.float32).at[q_lo:q_hi].set(ATTN_SCALE)
    chunked = lambda a: a.reshape(a.shape[0], 2 * N_FC, FC).transpose(1, 0, 2)
    return dict(
        g_mix=g_mix[l][None],
        w_in=(w_in[l] * col_scale).astype(bf),
        sgu_g=sgu_g[l][None],
        sgu_w=sgu_w[l].astype(bf),
        sgu_bfull=jnp.repeat(sgu_b[l].T, CHUNK, axis=1),
        lam_qk=lam_qk[l],
        sub_g=sub_g[l][None],
        w_br_a=w_br[l, 0].astype(bf),
        w_br_b=w_br[l, 1].astype(bf),
        w_out=w_out[l].astype(bf),
        g_ffn=g_ffn[l][None],
        w_up=chunked(w_up[l]).astype(bf),
        conv_w=chunked(conv_w[l]),
        conv_b=chunked(conv_b[l][None]),
        w_down=w_down[l].reshape(N_FC, FC, D_MODEL).astype(bf),
    )


def _trunk(x, band, layers, g_final):
    for l, p in enumerate(layers):
        lam_init = 0.8 - 0.6 * math.exp(-0.3 * l)
        ya, q, k, v, gates = _proj_call(x, p["g_mix"], p["w_in"], p["sgu_g"], p["sgu_w"],
                                        p["sgu_bfull"])
        yb = _attn_call(q, k, v, band, p["lam_qk"], p["sub_g"], lam_init)
        x = _merge_call(x, ya, yb, gates, p["w_br_a"], p["w_br_b"], p["w_out"])
        x = _ffn_call(x, p["g_ffn"], p["w_up"], p["conv_w"], p["conv_b"], p["w_down"],
                      g_final, final=(l == len(layers) - 1))
    return x


def kernel(x_prompt, x_sample, rel_bias, g_mix, w_in, sgu_g, sgu_w, sgu_b, lam_qk, sub_g,
           w_br, w_out, g_ffn, w_up, conv_w, conv_b, w_down, g_final):
    band = _bias_band(rel_bias)
    layers = [_prep_layer(l, g_mix, w_in, sgu_g, sgu_w, sgu_b, lam_qk, sub_g, w_br, w_out,
                          g_ffn, w_up, conv_w, conv_b, w_down) for l in range(DEPTH)]
    g_fin = g_final[None]
    return (_trunk(x_prompt, band, layers, g_fin), _trunk(x_sample, band, layers, g_fin))
```

```python
import functools
import math

import jax
import jax.numpy as jnp
from jax import lax
from jax.experimental import pallas as pl
from jax.experimental.pallas import tpu as pltpu

D_MODEL = 1024
DEPTH = 2
CHUNK = 128
A_GROUPS = 8
A_WIDTH = D_MODEL
B_HEADS = 8
B_HEAD_DIM = D_MODEL // (2 * B_HEADS)
HEAD_W = 2 * B_HEAD_DIM
B_WIDTH = B_HEADS * HEAD_W
ATTN_SCALE = B_HEAD_DIM ** -0.5
N_BUCKETS = 32
MAX_DIST = 128
D_FF = 2816
CONV_W = 3
EPS = 1e-6
IN_COLS = 2 * A_WIDTH + 3 * B_WIDTH + 2 * D_MODEL

V7X_LANES = 128
V7X_SUBLANES = 8
V7X_MXU_DIM = 256
V7X_VMEM_BYTES = 64 * 1024 * 1024

TM_PROJ = 512
TM_MERGE = 512
TM_FFN = 512
HALO = V7X_SUBLANES
FC = V7X_MXU_DIM
N_FC = D_FF // FC
FFN_RB = 64
ATTN_RB = 64
TQ = V7X_MXU_DIM
CK = V7X_MXU_DIM
N_BAND = 5
VT_ROWS = HEAD_W + 2 * V7X_SUBLANES
LOG2E = math.log2(math.e)

assert D_FF % FC == 0
assert MAX_DIST <= CK and MAX_DIST <= TQ


def _vmem_limit(nbytes):
    return int(min(nbytes, V7X_VMEM_BYTES - 4 * 1024 * 1024))


def _rms(x, g):
    return x * lax.rsqrt(jnp.mean(x * x, axis=-1, keepdims=True) + EPS) * g


def _skewed_pipeline(n, stage_a, stage_b, stage_c, unroll=False):
    assert n >= 2

    def run(*stages):
        order = sorted(((i + 0.5) / len(pieces), si, i)
                       for si, (pieces, _) in enumerate(stages) for i in range(len(pieces)))
        for _, si, i in order:
            stages[si][0][i]()
        return [result() for _, result in stages]

    def step(t, parity, ra, rb):
        ra_new, rb_new, _ = run(stage_a(t, parity), stage_b(t - 1, 1 - parity, ra),
                                stage_c(t - 2, parity, rb))
        return ra_new, rb_new

    ra0, = run(stage_a(0, 0))
    ra, rb = run(stage_a(1, 1), stage_b(0, 0, ra0))
    t0 = 2
    if (n - t0) % 2:
        ra, rb = step(t0, t0 % 2, ra, rb)
        t0 += 1

    def body(k, carry):
        t = t0 + 2 * k
        carry = step(t, t0 % 2, *carry)
        return step(t + 1, (t0 + 1) % 2, *carry)

    if unroll:
        for k in range((n - t0) // 2):
            ra, rb = body(k, (ra, rb))
    else:
        ra, rb = lax.fori_loop(0, (n - t0) // 2, body, (ra, rb))
    rb_last, _ = run(stage_b(n - 1, (n - 1) % 2, ra), stage_c(n - 2, n % 2, rb))
    run(stage_c(n - 1, (n - 1) % 2, rb_last))


def _rel_bucket(rel):
    nb = N_BUCKETS // 2
    max_exact = nb // 2
    ret = jnp.where(rel > 0, nb, 0)
    n = jnp.abs(rel)
    nf = jnp.maximum(n, 1).astype(jnp.float32)
    large = max_exact + (jnp.log(nf / max_exact) / math.log(MAX_DIST / max_exact)
                         * (nb - max_exact)).astype(jnp.int32)
    large = jnp.minimum(large, nb - 1)
    return ret + jnp.where(n < max_exact, n, large)


def _band_kernel(table_ref, idx_ref, out_ref):
    h = pl.program_id(0)
    idx = idx_ref[...]
    acc = jnp.zeros(idx.shape, jnp.float32)
    for b in range(N_BUCKETS):
        acc = jnp.where(idx == b, table_ref[b, h] * LOG2E, acc)
    out_ref[0] = acc


def _bias_band(rel_bias):
    d = jnp.arange(N_BAND, dtype=jnp.int32)[:, None, None] - N_BAND // 2
    kk = jnp.arange(CK, dtype=jnp.int32)[None, :, None]
    qq = jnp.arange(TQ, dtype=jnp.int32)[None, None, :]
    idx = _rel_bucket(d * CK + kk - qq)
    return pl.pallas_call(
        _band_kernel,
        out_shape=jax.ShapeDtypeStruct((B_HEADS, N_BAND, CK, TQ), jnp.float32),
        grid=(B_HEADS,),
        in_specs=[
            pl.BlockSpec(memory_space=pltpu.SMEM),
            pl.BlockSpec((N_BAND, CK, TQ), lambda h: (0, 0, 0)),
        ],
        out_specs=pl.BlockSpec((1, N_BAND, CK, TQ), lambda h: (h, 0, 0, 0)),
        name="bias_band",
    )(rel_bias, idx)


def _proj_kernel(x_ref, g_ref, w_ref, sg_ref, ws_ref, bs_ref,
                 ya_ref, q_ref, k_ref, v_ref, gate_ref):
    tm = x_ref.shape[0]
    h = _rms(x_ref[...], g_ref[...]).astype(jnp.bfloat16)

    def proj(lo, hi):
        return jnp.dot(h, w_ref[:, lo:hi], preferred_element_type=jnp.float32)

    u = proj(0, A_WIDTH)
    vn = _rms(proj(A_WIDTH, 2 * A_WIDTH), sg_ref[...]).astype(jnp.bfloat16)
    for n in range(tm // CHUNK):
        rows = slice(n * CHUNK, (n + 1) * CHUNK)
        for g in range(A_GROUPS):
            cols = slice(g * CHUNK, (g + 1) * CHUNK)
            mixed = jnp.dot(ws_ref[g], vn[rows, cols], preferred_element_type=jnp.float32)
            ya_ref[rows, cols] = (u[rows, cols] * (mixed + bs_ref[:, cols])).astype(ya_ref.dtype)

    base = 2 * A_WIDTH
    for i, o_ref in enumerate((q_ref, k_ref, v_ref)):
        r = proj(base + i * B_WIDTH, base + (i + 1) * B_WIDTH)
        for hd in range(B_HEADS):
            o_ref[0, hd] = r[:, hd * HEAD_W:(hd + 1) * HEAD_W].astype(o_ref.dtype)
    base += 3 * B_WIDTH
    gate_ref[...] = proj(base, base + 2 * D_MODEL).astype(gate_ref.dtype)


def _proj_call(x, g_mix, w_in, sgu_g, sgu_w, sgu_bfull):
    bsz, s, _ = x.shape
    m = bsz * s
    tm = TM_PROJ
    tiles_per_seq = s // tm
    const = lambda shape: pl.BlockSpec(shape, lambda i: (0,) * len(shape),
                                       pipeline_mode=pl.Buffered(1))
    head_spec = pl.BlockSpec((1, B_HEADS, tm, HEAD_W),
                             lambda i: (i // tiles_per_seq, 0, i % tiles_per_seq, 0))
    head_shape = jax.ShapeDtypeStruct((bsz, B_HEADS, s, HEAD_W), jnp.bfloat16)
    return pl.pallas_call(
        _proj_kernel,
        out_shape=(jax.ShapeDtypeStruct((m, A_WIDTH), jnp.bfloat16),
                   head_shape, head_shape, head_shape,
                   jax.ShapeDtypeStruct((m, 2 * D_MODEL), jnp.bfloat16)),
        grid=(m // tm,),
        in_specs=[
            pl.BlockSpec((tm, D_MODEL), lambda i: (i, 0)),
            const((1, D_MODEL)),
            const((D_MODEL, IN_COLS)),
            const((1, A_WIDTH)),
            const((A_GROUPS, CHUNK, CHUNK)),
            const((CHUNK, A_WIDTH)),
        ],
        out_specs=(pl.BlockSpec((tm, A_WIDTH), lambda i: (i, 0)),
                   head_spec, head_spec, head_spec,
                   pl.BlockSpec((tm, 2 * D_MODEL), lambda i: (i, 0))),
        compiler_params=pltpu.CompilerParams(
            dimension_semantics=("parallel",),
            vmem_limit_bytes=_vmem_limit(56 * 1024 * 1024)),
        name="proj_sgu",
    )(x.reshape(m, D_MODEL), g_mix, w_in, sgu_g, sgu_w, sgu_bfull)


def _attn_kernel(lam_init, q_ref, k_ref, v_ref, band_ref, lq_ref, subg_ref, o_ref,
                 vt_ref, l0_ref, l1_ref, p0_ref, p1_ref):
    l_refs, p_refs = (l0_ref, l1_ref), (p0_ref, p1_ref)
    s = q_ref.shape[2]
    nq, nk = s // TQ, s // CK
    rg = ATTN_RB // V7X_SUBLANES

    lq = lq_ref[...]
    lam = (jnp.exp(jnp.sum(lq[0:1] * lq[1:2], axis=-1, keepdims=True))
           - jnp.exp(jnp.sum(lq[2:3] * lq[3:4], axis=-1, keepdims=True)) + lam_init)

    for c in range(nk):
        cols = slice(c * CK, (c + 1) * CK)
        vt_ref[:HEAD_W, cols] = v_ref[0, 0, cols, :].astype(jnp.float32).T.astype(vt_ref.dtype)
    pad_row = lax.broadcasted_iota(jnp.int32, (VT_ROWS - HEAD_W, s), 0)
    vt_ref[HEAD_W:, :] = jnp.where(pad_row == 0, 1.0, 0.0).astype(vt_ref.dtype)

    lane = lax.broadcasted_iota(jnp.int32, (TQ, HEAD_W), 1)
    nt_dims = (((1,), (1,)), ((), ()))

    def q_rows(qi):
        return pl.ds(qi * TQ, TQ) if isinstance(qi, int) else pl.ds(pl.multiple_of(qi * TQ, TQ), TQ)

    def band_index(c, qi):
        lo, hi = -(N_BAND // 2), N_BAND // 2
        if isinstance(qi, int):
            return min(max(c - qi, lo), hi) - lo
        return jnp.clip(c - qi, lo, hi) - lo

    def logit_stage(qi, slot):
        l_ref = l_refs[slot]
        qm = []
        mx = [None, None]

        def piece(m, c):
            def run():
                if not qm:
                    q = q_ref[0, 0, q_rows(qi), :].astype(jnp.float32)
                    qm.append(jnp.where(lane < B_HEAD_DIM, q, 0.0).astype(jnp.bfloat16))
                    qm.append(jnp.where(lane >= B_HEAD_DIM, q, 0.0).astype(jnp.bfloat16))
                logit = lax.dot_general(k_ref[0, 0, c * CK:(c + 1) * CK, :], qm[m], nt_dims,
                                        preferred_element_type=jnp.float32)
                d = band_index(c, qi)
                for r in range(0, CK, ATTN_RB):
                    rows = slice(r, r + ATTN_RB)
                    lc = logit[rows] + band_ref[0, d, rows, :]
                    l_ref[m, c, rows, :] = lc
                    cm = jnp.max(lc.reshape(rg, V7X_SUBLANES, TQ), axis=0)
                    mx[m] = cm if mx[m] is None else jnp.maximum(mx[m], cm)
            return run

        pieces = [piece(m, c) for c in range(nk) for m in range(2)]
        return pieces, lambda: tuple(jnp.max(v, axis=0, keepdims=True) for v in mx)

    def exp_stage(qi, slot, mrow):
        l_ref, p_ref = l_refs[slot], p_refs[slot]

        def piece(m, r):
            def run():
                x = l_ref[m, r // CK, r % CK:r % CK + ATTN_RB, :] - mrow[m]
                p_ref[m, r:r + ATTN_RB, :] = jnp.exp2(x.astype(p_ref.dtype))
            return run

        return [piece(m, r) for m in range(2) for r in range(0, s, ATTN_RB)], lambda: ()

    def value_stage(qi, slot, _):
        acc = []

        def matmul(m):
            def run():
                acc.append(jnp.dot(vt_ref[...], p_refs[slot][m],
                                   preferred_element_type=jnp.float32))
            return run

        def finish():
            sums = [a[HEAD_W:HEAD_W + 1] for a in acc]
            o = acc[0][:HEAD_W] / sums[0] - acc[1][:HEAD_W] * (lam / sums[1])
            o = o * lax.rsqrt(jnp.mean(o * o, axis=0, keepdims=True) + EPS)
            o = o.T * subg_ref[...] * (1.0 - lam_init)
            o_ref[0, 0, q_rows(qi), :] = o.astype(o_ref.dtype)

        return [matmul(0), matmul(1), finish], lambda: ()

    _skewed_pipeline(nq, logit_stage, exp_stage, value_stage)


def _attn_call(q, k, v, band, lam_qk, sub_g, lam_init):
    bsz, _, s, _ = q.shape
    head = pl.BlockSpec((1, 1, s, HEAD_W), lambda b, h: (b, h, 0, 0))
    return pl.pallas_call(
        functools.partial(_attn_kernel, lam_init),
        out_shape=jax.ShapeDtypeStruct(q.shape, jnp.bfloat16),
        grid=(bsz, B_HEADS),
        in_specs=[
            head, head, head,
            pl.BlockSpec((1, N_BAND, CK, TQ), lambda b, h: (h, 0, 0, 0)),
            pl.BlockSpec((4, B_HEAD_DIM), lambda b, h: (0, 0)),
            pl.BlockSpec((1, HEAD_W), lambda b, h: (0, 0)),
        ],
        out_specs=head,
        scratch_shapes=[
            pltpu.VMEM((VT_ROWS, s), jnp.bfloat16),
            pltpu.VMEM((2, s // CK, CK, TQ), jnp.float32),
            pltpu.VMEM((2, s // CK, CK, TQ), jnp.float32),
            pltpu.VMEM((2, s, TQ), jnp.bfloat16),
            pltpu.VMEM((2, s, TQ), jnp.bfloat16),
        ],
        compiler_params=pltpu.CompilerParams(
            dimension_semantics=("parallel", "parallel"),
            vmem_limit_bytes=_vmem_limit(32 * 1024 * 1024)),
        name="diff_attn",
    )(q, k, v, band, lam_qk, sub_g)


def _merge_kernel(x_ref, ya_ref, yb_ref, gate_ref, wa_ref, wb_ref, wo_ref, o_ref):
    yb = jnp.concatenate([yb_ref[0, hd] for hd in range(B_HEADS)], axis=-1)
    br_a = jnp.dot(ya_ref[...], wa_ref[...], preferred_element_type=jnp.float32)
    br_b = jnp.dot(yb, wb_ref[...], preferred_element_type=jnp.float32)
    gate = gate_ref[...].astype(jnp.float32)
    merged = (jax.nn.sigmoid(gate[:, :D_MODEL]) * br_a
              + jax.nn.sigmoid(gate[:, D_MODEL:]) * br_b)
    o_ref[...] = x_ref[...] + jnp.dot(merged.astype(jnp.bfloat16), wo_ref[...],
                                      preferred_element_type=jnp.float32)


def _merge_call(x, ya, yb, gates, w_br_a, w_br_b, w_out):
    bsz, s, _ = x.shape
    m = bsz * s
    tm = TM_MERGE
    tiles_per_seq = s // tm
    row = lambda width: pl.BlockSpec((tm, width), lambda i: (i, 0))
    const = lambda shape: pl.BlockSpec(shape, lambda i: (0,) * len(shape),
                                       pipeline_mode=pl.Buffered(1))
    out = pl.pallas_call(
        _merge_kernel,
        out_shape=jax.ShapeDtypeStruct((m, D_MODEL), jnp.float32),
        grid=(m // tm,),
        in_specs=[
            row(D_MODEL), row(A_WIDTH),
            pl.BlockSpec((1, B_HEADS, tm, HEAD_W),
                         lambda i: (i // tiles_per_seq, 0, i % tiles_per_seq, 0)),
            row(2 * D_MODEL),
            const((A_WIDTH, D_MODEL)), const((B_WIDTH, D_MODEL)), const((D_MODEL, D_MODEL)),
        ],
        out_specs=row(D_MODEL),
        compiler_params=pltpu.CompilerParams(
            dimension_semantics=("parallel",),
            vmem_limit_bytes=_vmem_limit(48 * 1024 * 1024)),
        name="merge_out",
    )(x.reshape(m, D_MODEL), ya, yb, gates, w_br_a, w_br_b, w_out)
    return out.reshape(bsz, s, D_MODEL)


def _ffn_kernel(final, x_ref, xp_ref, xn_ref, g_ref, wup_ref, cw_ref, cb_ref, wdn_ref, gf_ref,
                o_ref, h_ref, up0_ref, up1_ref, gated0_ref, gated1_ref, acc_ref):
    up_refs, gated_refs = (up0_ref, up1_ref), (gated0_ref, gated1_ref)
    tm = x_ref.shape[1]
    j = pl.program_id(1)
    x = x_ref[0]
    keep_p = (j > 0).astype(jnp.float32)
    keep_n = (j < pl.num_programs(1) - 1).astype(jnp.float32)
    g = g_ref[...]
    h_ref[...] = jnp.concatenate(
        [_rms(xp_ref[0], g) * keep_p, _rms(x, g), _rms(xn_ref[0], g) * keep_n],
        axis=0).astype(h_ref.dtype)
    acc_ref[...] = jnp.zeros(acc_ref.shape, acc_ref.dtype)

    nothing = lambda: ()

    def up_stage(c, slot):
        def piece(half):
            def run():
                up_refs[slot][half] = jnp.dot(h_ref[...], wup_ref[half * N_FC + c],
                                              preferred_element_type=jnp.float32)
            return run
        return [piece(0), piece(1)], nothing

    def conv(up_ref, half, cc, r0):
        n = FFN_RB + 2 * HALO
        ext = up_ref[half, r0:r0 + n, :]
        w = cw_ref[cc]
        prev = pltpu.roll(ext, 1, 0)
        nxt = pltpu.roll(ext, n - 1, 0)
        mid = slice(HALO, HALO + FFN_RB)
        return prev[mid] * w[0:1] + ext[mid] * w[1:2] + nxt[mid] * w[2:3] + cb_ref[cc]

    def gate_stage(c, slot, _):
        def piece(r0):
            def run():
                a = conv(up_refs[slot], 0, c, r0)
                b = conv(up_refs[slot], 1, N_FC + c, r0)
                gated_refs[slot][r0:r0 + FFN_RB, :] = (jax.nn.gelu(a) * b).astype(jnp.bfloat16)
            return run
        return [piece(r0) for r0 in range(0, tm, FFN_RB)], nothing

    def down_stage(c, slot, _):
        def piece(n0):
            def run():
                cols = slice(n0, n0 + V7X_MXU_DIM)
                acc_ref[:, cols] += jnp.dot(gated_refs[slot][...], wdn_ref[c, :, cols],
                                            preferred_element_type=jnp.float32)
            return run
        return [piece(n0) for n0 in range(0, D_MODEL, V7X_MXU_DIM)], nothing

    _skewed_pipeline(N_FC, up_stage, gate_stage, down_stage, unroll=True)
    y = x + acc_ref[...]
    if final:
        y = _rms(y, gf_ref[...])
    o_ref[0] = y


def _ffn_call(x, g_ffn, w_up, conv_w, conv_b, w_down, g_final, final):
    bsz, s, _ = x.shape
    tm = TM_FFN
    nt = s // tm
    hb = tm // HALO
    const = lambda shape: pl.BlockSpec(shape, lambda b, j: (0,) * len(shape),
                                       pipeline_mode=pl.Buffered(1))
    return pl.pallas_call(
        functools.partial(_ffn_kernel, final),
        out_shape=jax.ShapeDtypeStruct(x.shape, jnp.float32),
        grid=(bsz, nt),
        in_specs=[
            pl.BlockSpec((1, tm, D_MODEL), lambda b, j: (b, j, 0)),
            pl.BlockSpec((1, HALO, D_MODEL), lambda b, j: (b, jnp.maximum(j * hb - 1, 0), 0)),
            pl.BlockSpec((1, HALO, D_MODEL),
                         lambda b, j: (b, jnp.minimum((j + 1) * hb, nt * hb - 1), 0)),
            const((1, D_MODEL)),
            const((2 * N_FC, D_MODEL, FC)),
            const((2 * N_FC, CONV_W, FC)),
            const((2 * N_FC, 1, FC)),
            const((N_FC, FC, D_MODEL)),
            const((1, D_MODEL)),
        ],
        out_specs=pl.BlockSpec((1, tm, D_MODEL), lambda b, j: (b, j, 0)),
        scratch_shapes=[
            pltpu.VMEM((tm + 2 * HALO, D_MODEL), jnp.bfloat16),
            pltpu.VMEM((2, tm + 2 * HALO, FC), jnp.float32),
            pltpu.VMEM((2, tm + 2 * HALO, FC), jnp.float32),
            pltpu.VMEM((tm, FC), jnp.bfloat16),
            pltpu.VMEM((tm, FC), jnp.bfloat16),
            pltpu.VMEM((tm, D_MODEL), jnp.float32),
        ],
        compiler_params=pltpu.CompilerParams(
            dimension_semantics=("parallel", "parallel"),
            vmem_limit_bytes=_vmem_limit(56 * 1024 * 1024)),
        name="conv_ffn",
    )(x, x, x, g_ffn, w_up, conv_w, conv_b, w_down, g_final)


def _prep_layer(l, g_mix, w_in, sgu_g, sgu_w, sgu_b, lam_qk, sub_g, w_br, w_out,
                g_ffn, w_up, conv_w, conv_b, w_down):
    bf = jnp.bfloat16
    q_lo, q_hi = 2 * A_WIDTH, 2 * A_WIDTH + B_WIDTH
    col_scale = jnp.ones((IN_COLS,), jnp.float32).at[q_lo:q_hi].set(ATTN_SCALE * LOG2E)
    chunked = lambda a: a.reshape(a.shape[0], 2 * N_FC, FC).transpose(1, 0, 2)
    return dict(
        g_mix=g_mix[l][None],
        w_in=(w_in[l] * col_scale).astype(bf),
        sgu_g=sgu_g[l][None],
        sgu_w=sgu_w[l].astype(bf),
        sgu_bfull=jnp.repeat(sgu_b[l].T, CHUNK, axis=1),
        lam_qk=lam_qk[l],
        sub_g=sub_g[l][None],
        w_br_a=w_br[l, 0].astype(bf),
        w_br_b=w_br[l, 1].astype(bf),
        w_out=w_out[l].astype(bf),
        g_ffn=g_ffn[l][None],
        w_up=chunked(w_up[l]).astype(bf),
        conv_w=chunked(conv_w[l]),
        conv_b=chunked(conv_b[l][None]),
        w_down=w_down[l].reshape(N_FC, FC, D_MODEL).astype(bf),
    )


def _trunk(x, band, layers, g_final):
    for l, p in enumerate(layers):
        lam_init = 0.8 - 0.6 * math.exp(-0.3 * l)
        ya, q, k, v, gates = _proj_call(x, p["g_mix"], p["w_in"], p["sgu_g"], p["sgu_w"],
                                        p["sgu_bfull"])
        yb = _attn_call(q, k, v, band, p["lam_qk"], p["sub_g"], lam_init)
        x = _merge_call(x, ya, yb, gates, p["w_br_a"], p["w_br_b"], p["w_out"])
        x = _ffn_call(x, p["g_ffn"], p["w_up"], p["conv_w"], p["conv_b"], p["w_down"],
                      g_final, final=(l == len(layers) - 1))
    return x


def kernel(x_prompt, x_sample, rel_bias, g_mix, w_in, sgu_g, sgu_w, sgu_b, lam_qk, sub_g,
           w_br, w_out, g_ffn, w_up, conv_w, conv_b, w_down, g_final):
    band = _bias_band(rel_bias)
    layers = [_prep_layer(l, g_mix, w_in, sgu_g, sgu_w, sgu_b, lam_qk, sub_g, w_br, w_out,
                          g_ffn, w_up, conv_w, conv_b, w_down) for l in range(DEPTH)]
    g_fin = g_final[None]
    return (_trunk(x_prompt, band, layers, g_fin), _trunk(x_sample, band, layers, g_fin))
```

```python
import functools
import math

import jax
import jax.numpy as jnp
from jax import lax
from jax.experimental import pallas as pl
from jax.experimental.pallas import tpu as pltpu

D_MODEL = 1024
DEPTH = 2
CHUNK = 128
A_GROUPS = 8
A_WIDTH = D_MODEL
B_HEADS = 8
B_HEAD_DIM = D_MODEL // (2 * B_HEADS)
HEAD_W = 2 * B_HEAD_DIM
B_WIDTH = B_HEADS * HEAD_W
ATTN_SCALE = B_HEAD_DIM ** -0.5
N_BUCKETS = 32
MAX_DIST = 128
D_FF = 2816
CONV_W = 3
EPS = 1e-6
IN_COLS = 2 * A_WIDTH + 3 * B_WIDTH + 2 * D_MODEL

V7X_LANES = 128
V7X_SUBLANES = 8
V7X_MXU_DIM = 256
V7X_VMEM_BYTES = 64 * 1024 * 1024

TM_PROJ = 512
TM_MERGE = 512
TM_FFN = 512
HALO = V7X_SUBLANES
FC = V7X_MXU_DIM
N_FC = D_FF // FC
FFN_RB = 64
ATTN_RB = 64
ATTN_SLOTS = 3
TQ = V7X_MXU_DIM
CK = V7X_MXU_DIM
N_BAND = 5
VT_ROWS = HEAD_W + 2 * V7X_SUBLANES
LOG2E = math.log2(math.e)
GELU_C0 = math.sqrt(2.0 / math.pi)
GELU_C1 = 0.044715 * GELU_C0

assert D_FF % FC == 0
assert MAX_DIST <= CK and MAX_DIST <= TQ


def _vmem_limit(nbytes):
    return int(min(nbytes, V7X_VMEM_BYTES - 4 * 1024 * 1024))


def _rms(x, g):
    return x * lax.rsqrt(jnp.mean(x * x, axis=-1, keepdims=True) + EPS) * g


def _run_interleaved(*stages):
    order = sorted(((i + 0.5) / len(pieces), si, i)
                   for si, (pieces, _) in enumerate(stages) for i in range(len(pieces)))
    for _, si, i in order:
        stages[si][0][i]()
    return [result() for _, result in stages]


def _skewed_pipeline(n, stage_a, stage_b, stage_c, slots=2, unroll=False):
    assert n >= 2 and slots >= 2
    run = _run_interleaved

    def step(t, t_static, ra, rb):
        ra_new, rb_new, _ = run(stage_a(t, t_static % slots),
                                stage_b(t - 1, (t_static - 1) % slots, ra),
                                stage_c(t - 2, (t_static - 2) % slots, rb))
        return ra_new, rb_new

    ra0, = run(stage_a(0, 0))
    ra, rb = run(stage_a(1, 1), stage_b(0, 0, ra0))
    t0 = 2
    while (n - t0) % slots:
        ra, rb = step(t0, t0, ra, rb)
        t0 += 1

    def body(k, carry):
        for j in range(slots):
            carry = step(t0 + slots * k + j, t0 + j, *carry)
        return carry

    if unroll:
        for k in range((n - t0) // slots):
            ra, rb = body(k, (ra, rb))
    else:
        ra, rb = lax.fori_loop(0, (n - t0) // slots, body, (ra, rb))
    rb_last, _ = run(stage_b(n - 1, (n - 1) % slots, ra), stage_c(n - 2, (n - 2) % slots, rb))
    run(stage_c(n - 1, (n - 1) % slots, rb_last))


def _rel_bucket(rel):
    nb = N_BUCKETS // 2
    max_exact = nb // 2
    ret = jnp.where(rel > 0, nb, 0)
    n = jnp.abs(rel)
    nf = jnp.maximum(n, 1).astype(jnp.float32)
    large = max_exact + (jnp.log(nf / max_exact) / math.log(MAX_DIST / max_exact)
                         * (nb - max_exact)).astype(jnp.int32)
    large = jnp.minimum(large, nb - 1)
    return ret + jnp.where(n < max_exact, n, large)


def _band_kernel(table_ref, idx_ref, out_ref):
    h = pl.program_id(0)
    idx = idx_ref[...]
    acc = jnp.zeros(idx.shape, jnp.float32)
    for b in range(N_BUCKETS):
        acc = jnp.where(idx == b, table_ref[b, h] * LOG2E, acc)
    out_ref[0] = acc


def _bias_band(rel_bias):
    d = jnp.arange(N_BAND, dtype=jnp.int32)[:, None, None] - N_BAND // 2
    kk = jnp.arange(CK, dtype=jnp.int32)[None, :, None]
    qq = jnp.arange(TQ, dtype=jnp.int32)[None, None, :]
    idx = _rel_bucket(d * CK + kk - qq)
    return pl.pallas_call(
        _band_kernel,
        out_shape=jax.ShapeDtypeStruct((B_HEADS, N_BAND, CK, TQ), jnp.float32),
        grid=(B_HEADS,),
        in_specs=[
            pl.BlockSpec(memory_space=pltpu.SMEM),
            pl.BlockSpec((N_BAND, CK, TQ), lambda h: (0, 0, 0)),
        ],
        out_specs=pl.BlockSpec((1, N_BAND, CK, TQ), lambda h: (h, 0, 0, 0)),
        name="bias_band",
    )(rel_bias, idx)


def _proj_kernel(x_ref, g_ref, w_ref, sg_ref, ws_ref, bs_ref,
                 ya_ref, q_ref, k_ref, v_ref, gate_ref):
    tm = x_ref.shape[0]
    h = _rms(x_ref[...], g_ref[...]).astype(jnp.bfloat16)

    def proj(lo, hi):
        return jnp.dot(h, w_ref[:, lo:hi], preferred_element_type=jnp.float32)

    u = proj(0, A_WIDTH)
    vn = _rms(proj(A_WIDTH, 2 * A_WIDTH), sg_ref[...]).astype(jnp.bfloat16)
    for n in range(tm // CHUNK):
        rows = slice(n * CHUNK, (n + 1) * CHUNK)
        for g in range(A_GROUPS):
            cols = slice(g * CHUNK, (g + 1) * CHUNK)
            mixed = jnp.dot(ws_ref[g], vn[rows, cols], preferred_element_type=jnp.float32)
            ya_ref[rows, cols] = (u[rows, cols] * (mixed + bs_ref[:, cols])).astype(ya_ref.dtype)

    base = 2 * A_WIDTH
    for i, o_ref in enumerate((q_ref, k_ref, v_ref)):
        r = proj(base + i * B_WIDTH, base + (i + 1) * B_WIDTH)
        for hd in range(B_HEADS):
            o_ref[0, hd] = r[:, hd * HEAD_W:(hd + 1) * HEAD_W].astype(o_ref.dtype)
    base += 3 * B_WIDTH
    gate_ref[...] = proj(base, base + 2 * D_MODEL).astype(gate_ref.dtype)


def _proj_call(x, g_mix, w_in, sgu_g, sgu_w, sgu_bfull):
    bsz, s, _ = x.shape
    m = bsz * s
    tm = TM_PROJ
    tiles_per_seq = s // tm
    const = lambda shape: pl.BlockSpec(shape, lambda i: (0,) * len(shape),
                                       pipeline_mode=pl.Buffered(1))
    head_spec = pl.BlockSpec((1, B_HEADS, tm, HEAD_W),
                             lambda i: (i // tiles_per_seq, 0, i % tiles_per_seq, 0))
    head_shape = jax.ShapeDtypeStruct((bsz, B_HEADS, s, HEAD_W), jnp.bfloat16)
    return pl.pallas_call(
        _proj_kernel,
        out_shape=(jax.ShapeDtypeStruct((m, A_WIDTH), jnp.bfloat16),
                   head_shape, head_shape, head_shape,
                   jax.ShapeDtypeStruct((m, 2 * D_MODEL), jnp.bfloat16)),
        grid=(m // tm,),
        in_specs=[
            pl.BlockSpec((tm, D_MODEL), lambda i: (i, 0)),
            const((1, D_MODEL)),
            const((D_MODEL, IN_COLS)),
            const((1, A_WIDTH)),
            const((A_GROUPS, CHUNK, CHUNK)),
            const((CHUNK, A_WIDTH)),
        ],
        out_specs=(pl.BlockSpec((tm, A_WIDTH), lambda i: (i, 0)),
                   head_spec, head_spec, head_spec,
                   pl.BlockSpec((tm, 2 * D_MODEL), lambda i: (i, 0))),
        compiler_params=pltpu.CompilerParams(
            dimension_semantics=("parallel",),
            vmem_limit_bytes=_vmem_limit(56 * 1024 * 1024)),
        name="proj_sgu",
    )(x.reshape(m, D_MODEL), g_mix, w_in, sgu_g, sgu_w, sgu_bfull)


def _attn_kernel(lam_init, q_ref, k_ref, v_ref, band_ref, lq_ref, subg_ref, o_ref,
                 vt_ref, *slot_refs):
    l_refs, p_refs = slot_refs[:ATTN_SLOTS], slot_refs[ATTN_SLOTS:]
    s = q_ref.shape[2]
    nq, nk = s // TQ, s // CK
    rg = ATTN_RB // V7X_SUBLANES

    lq = lq_ref[...]
    lam = (jnp.exp(jnp.sum(lq[0:1] * lq[1:2], axis=-1, keepdims=True))
           - jnp.exp(jnp.sum(lq[2:3] * lq[3:4], axis=-1, keepdims=True)) + lam_init)

    for c in range(nk):
        cols = slice(c * CK, (c + 1) * CK)
        vt_ref[:HEAD_W, cols] = v_ref[0, 0, cols, :].astype(jnp.float32).T.astype(vt_ref.dtype)
    pad_row = lax.broadcasted_iota(jnp.int32, (VT_ROWS - HEAD_W, s), 0)
    vt_ref[HEAD_W:, :] = jnp.where(pad_row == 0, 1.0, 0.0).astype(vt_ref.dtype)

    lane = lax.broadcasted_iota(jnp.int32, (TQ, HEAD_W), 1)
    nt_dims = (((1,), (1,)), ((), ()))

    def q_rows(qi):
        return pl.ds(qi * TQ, TQ) if isinstance(qi, int) else pl.ds(pl.multiple_of(qi * TQ, TQ), TQ)

    def band_index(c, qi):
        lo, hi = -(N_BAND // 2), N_BAND // 2
        if isinstance(qi, int):
            return min(max(c - qi, lo), hi) - lo
        return jnp.clip(c - qi, lo, hi) - lo

    def logit_stage(qi, slot):
        l_ref = l_refs[slot]
        qm = []
        mx = [None, None]

        def piece(m, c):
            def run():
                if not qm:
                    q = q_ref[0, 0, q_rows(qi), :].astype(jnp.float32)
                    qm.append(jnp.where(lane < B_HEAD_DIM, q, 0.0).astype(jnp.bfloat16))
                    qm.append(jnp.where(lane >= B_HEAD_DIM, q, 0.0).astype(jnp.bfloat16))
                logit = lax.dot_general(k_ref[0, 0, c * CK:(c + 1) * CK, :], qm[m], nt_dims,
                                        preferred_element_type=jnp.float32)
                d = band_index(c, qi)
                for r in range(0, CK, ATTN_RB):
                    rows = slice(r, r + ATTN_RB)
                    lc = logit[rows] + band_ref[0, d, rows, :]
                    l_ref[m, c, rows, :] = lc
                    cm = jnp.max(lc.reshape(rg, V7X_SUBLANES, TQ), axis=0)
                    mx[m] = cm if mx[m] is None else jnp.maximum(mx[m], cm)
            return run

        pieces = [piece(m, c) for c in range(nk) for m in range(2)]
        return pieces, lambda: tuple(jnp.max(v, axis=0, keepdims=True) for v in mx)

    def exp_stage(qi, slot, mrow):
        l_ref, p_ref = l_refs[slot], p_refs[slot]

        def piece(m, r):
            def run():
                x = l_ref[m, r // CK, r % CK:r % CK + ATTN_RB, :] - mrow[m]
                p_ref[m, r:r + ATTN_RB, :] = jnp.exp2(x.astype(p_ref.dtype))
            return run

        return [piece(m, r) for m in range(2) for r in range(0, s, ATTN_RB)], lambda: ()

    def value_stage(qi, slot, _):
        acc = []

        def matmul(m):
            def run():
                acc.append(jnp.dot(vt_ref[...], p_refs[slot][m],
                                   preferred_element_type=jnp.float32))
            return run

        def finish():
            sums = [a[HEAD_W:HEAD_W + 1] for a in acc]
            o = acc[0][:HEAD_W] / sums[0] - acc[1][:HEAD_W] * (lam / sums[1])
            o = o * lax.rsqrt(jnp.mean(o * o, axis=0, keepdims=True) + EPS)
            o = o.T * subg_ref[...] * (1.0 - lam_init)
            o_ref[0, 0, q_rows(qi), :] = o.astype(o_ref.dtype)

        return [matmul(0), matmul(1), finish], lambda: ()

    _skewed_pipeline(nq, logit_stage, exp_stage, value_stage, slots=ATTN_SLOTS)


def _attn_call(q, k, v, band, lam_qk, sub_g, lam_init):
    bsz, _, s, _ = q.shape
    head = pl.BlockSpec((1, 1, s, HEAD_W), lambda b, h: (b, h, 0, 0))
    return pl.pallas_call(
        functools.partial(_attn_kernel, lam_init),
        out_shape=jax.ShapeDtypeStruct(q.shape, jnp.bfloat16),
        grid=(bsz, B_HEADS),
        in_specs=[
            head, head, head,
            pl.BlockSpec((1, N_BAND, CK, TQ), lambda b, h: (h, 0, 0, 0)),
            pl.BlockSpec((4, B_HEAD_DIM), lambda b, h: (0, 0)),
            pl.BlockSpec((1, HEAD_W), lambda b, h: (0, 0)),
        ],
        out_specs=head,
        scratch_shapes=[
            pltpu.VMEM((VT_ROWS, s), jnp.bfloat16),
            *[pltpu.VMEM((2, s // CK, CK, TQ), jnp.float32)] * ATTN_SLOTS,
            *[pltpu.VMEM((2, s, TQ), jnp.bfloat16)] * ATTN_SLOTS,
        ],
        compiler_params=pltpu.CompilerParams(
            dimension_semantics=("parallel", "parallel"),
            vmem_limit_bytes=_vmem_limit(48 * 1024 * 1024)),
        name="diff_attn",
    )(q, k, v, band, lam_qk, sub_g)


def _merge_kernel(x_ref, ya_ref, yb_ref, gate_ref, wa_ref, wb_ref, wo_ref, o_ref):
    yb = jnp.concatenate([yb_ref[0, hd] for hd in range(B_HEADS)], axis=-1)
    br_a = jnp.dot(ya_ref[...], wa_ref[...], preferred_element_type=jnp.float32)
    br_b = jnp.dot(yb, wb_ref[...], preferred_element_type=jnp.float32)
    gate = gate_ref[...].astype(jnp.float32)
    merged = (jax.nn.sigmoid(gate[:, :D_MODEL]) * br_a
              + jax.nn.sigmoid(gate[:, D_MODEL:]) * br_b)
    o_ref[...] = x_ref[...] + jnp.dot(merged.astype(jnp.bfloat16), wo_ref[...],
                                      preferred_element_type=jnp.float32)


def _merge_call(x, ya, yb, gates, w_br_a, w_br_b, w_out):
    bsz, s, _ = x.shape
    m = bsz * s
    tm = TM_MERGE
    tiles_per_seq = s // tm
    row = lambda width: pl.BlockSpec((tm, width), lambda i: (i, 0))
    const = lambda shape: pl.BlockSpec(shape, lambda i: (0,) * len(shape),
                                       pipeline_mode=pl.Buffered(1))
    out = pl.pallas_call(
        _merge_kernel,
        out_shape=jax.ShapeDtypeStruct((m, D_MODEL), jnp.float32),
        grid=(m // tm,),
        in_specs=[
            row(D_MODEL), row(A_WIDTH),
            pl.BlockSpec((1, B_HEADS, tm, HEAD_W),
                         lambda i: (i // tiles_per_seq, 0, i % tiles_per_seq, 0)),
            row(2 * D_MODEL),
            const((A_WIDTH, D_MODEL)), const((B_WIDTH, D_MODEL)), const((D_MODEL, D_MODEL)),
        ],
        out_specs=row(D_MODEL),
        compiler_params=pltpu.CompilerParams(
            dimension_semantics=("parallel",),
            vmem_limit_bytes=_vmem_limit(48 * 1024 * 1024)),
        name="merge_out",
    )(x.reshape(m, D_MODEL), ya, yb, gates, w_br_a, w_br_b, w_out)
    return out.reshape(bsz, s, D_MODEL)


def _ffn_kernel(final, x_ref, xp_ref, xn_ref, g_ref, wup_ref, cw_ref, cb_ref, wdn_ref, gf_ref,
                o_ref, h_ref, up0_ref, up1_ref, gated0_ref, gated1_ref, acc_ref):
    up_refs, gated_refs = (up0_ref, up1_ref), (gated0_ref, gated1_ref)
    tm = x_ref.shape[1]
    j = pl.program_id(1)
    x = x_ref[0]
    keep_p = (j > 0).astype(jnp.float32)
    keep_n = (j < pl.num_programs(1) - 1).astype(jnp.float32)
    g = g_ref[...]
    h_ref[...] = jnp.concatenate(
        [_rms(xp_ref[0], g) * keep_p, _rms(x, g), _rms(xn_ref[0], g) * keep_n],
        axis=0).astype(h_ref.dtype)
    acc_ref[...] = jnp.zeros(acc_ref.shape, acc_ref.dtype)

    nothing = lambda: ()

    def ff_cols(half, c):
        return slice(half * D_FF + c * FC, half * D_FF + (c + 1) * FC)

    def up_stage(c, slot):
        def piece(half):
            def run():
                up_refs[slot][half] = jnp.dot(h_ref[...], wup_ref[:, ff_cols(half, c)],
                                              preferred_element_type=jnp.float32)
            return run
        return [piece(0), piece(1)], nothing

    def conv(up_ref, half, c, r0):
        n = FFN_RB + 2 * HALO
        ext = up_ref[half, r0:r0 + n, :]
        w = cw_ref[:, ff_cols(half, c)]
        prev = pltpu.roll(ext, 1, 0)
        nxt = pltpu.roll(ext, n - 1, 0)
        mid = slice(HALO, HALO + FFN_RB)
        return (prev[mid] * w[0:1] + ext[mid] * w[1:2] + nxt[mid] * w[2:3]
                + cb_ref[:, ff_cols(half, c)])

    def gate_stage(c, slot, _):
        def piece(r0):
            def run():
                a = conv(up_refs[slot], 0, c, r0)
                half_b = conv(up_refs[slot], 1, c, r0)
                t = jnp.tanh(a * (GELU_C0 + GELU_C1 * (a * a)))
                gated_refs[slot][r0:r0 + FFN_RB, :] = ((a + a * t) * half_b).astype(jnp.bfloat16)
            return run
        return [piece(r0) for r0 in range(0, tm, FFN_RB)], nothing

    def down_stage(c, slot, _):
        def piece(n0):
            def run():
                cols = slice(n0, n0 + V7X_MXU_DIM)
                acc_ref[:, cols] += jnp.dot(gated_refs[slot][...],
                                            wdn_ref[c * FC:(c + 1) * FC, cols],
                                            preferred_element_type=jnp.float32)
            return run
        return [piece(n0) for n0 in range(0, D_MODEL, V7X_MXU_DIM)], nothing

    _skewed_pipeline(N_FC, up_stage, gate_stage, down_stage, unroll=True)
    y = x + acc_ref[...]
    if final:
        y = _rms(y, gf_ref[...])
    o_ref[0] = y


def _ffn_call(x, g_ffn, w_up, conv_w, conv_b, w_down, g_final, final):
    bsz, s, _ = x.shape
    tm = TM_FFN
    nt = s // tm
    hb = tm // HALO
    const = lambda shape: pl.BlockSpec(shape, lambda b, j: (0,) * len(shape),
                                       pipeline_mode=pl.Buffered(1))
    return pl.pallas_call(
        functools.partial(_ffn_kernel, final),
        out_shape=jax.ShapeDtypeStruct(x.shape, jnp.float32),
        grid=(bsz, nt),
        in_specs=[
            pl.BlockSpec((1, tm, D_MODEL), lambda b, j: (b, j, 0)),
            pl.BlockSpec((1, HALO, D_MODEL), lambda b, j: (b, jnp.maximum(j * hb - 1, 0), 0)),
            pl.BlockSpec((1, HALO, D_MODEL),
                         lambda b, j: (b, jnp.minimum((j + 1) * hb, nt * hb - 1), 0)),
            const((1, D_MODEL)),
            const((D_MODEL, 2 * D_FF)),
            const((CONV_W, 2 * D_FF)),
            const((1, 2 * D_FF)),
            const((D_FF, D_MODEL)),
            const((1, D_MODEL)),
        ],
        out_specs=pl.BlockSpec((1, tm, D_MODEL), lambda b, j: (b, j, 0)),
        scratch_shapes=[
            pltpu.VMEM((tm + 2 * HALO, D_MODEL), jnp.bfloat16),
            pltpu.VMEM((2, tm + 2 * HALO, FC), jnp.float32),
            pltpu.VMEM((2, tm + 2 * HALO, FC), jnp.float32),
            pltpu.VMEM((tm, FC), jnp.bfloat16),
            pltpu.VMEM((tm, FC), jnp.bfloat16),
            pltpu.VMEM((tm, D_MODEL), jnp.float32),
        ],
        compiler_params=pltpu.CompilerParams(
            dimension_semantics=("parallel", "parallel"),
            vmem_limit_bytes=_vmem_limit(56 * 1024 * 1024)),
        name="conv_ffn",
    )(x, x, x, g_ffn, w_up, conv_w, conv_b, w_down, g_final)


def _prep_layer(l, g_mix, w_in, sgu_g, sgu_w, sgu_b, lam_qk, sub_g, w_br, w_out,
                g_ffn, w_up, conv_w, conv_b, w_down):
    bf = jnp.bfloat16
    q_lo, q_hi = 2 * A_WIDTH, 2 * A_WIDTH + B_WIDTH
    col_scale = jnp.ones((IN_COLS,), jnp.float32).at[q_lo:q_hi].set(ATTN_SCALE * LOG2E)
    gate_scale = jnp.ones((2 * D_FF,), jnp.float32).at[D_FF:].set(0.5)
    return dict(
        g_mix=g_mix[l][None],
        w_in=(w_in[l] * col_scale).astype(bf),
        sgu_g=sgu_g[l][None],
        sgu_w=sgu_w[l].astype(bf),
        sgu_bfull=jnp.repeat(sgu_b[l].T, CHUNK, axis=1),
        lam_qk=lam_qk[l],
        sub_g=sub_g[l][None],
        w_br_a=w_br[l, 0].astype(bf),
        w_br_b=w_br[l, 1].astype(bf),
        w_out=w_out[l].astype(bf),
        g_ffn=g_ffn[l][None],
        w_up=w_up[l].astype(bf),
        conv_w=conv_w[l] * gate_scale,
        conv_b=(conv_b[l] * gate_scale)[None],
        w_down=w_down[l].astype(bf),
    )


def _trunk(x, band, layers, g_final):
    for l, p in enumerate(layers):
        lam_init = 0.8 - 0.6 * math.exp(-0.3 * l)
        ya, q, k, v, gates = _proj_call(x, p["g_mix"], p["w_in"], p["sgu_g"], p["sgu_w"],
                                        p["sgu_bfull"])
        yb = _attn_call(q, k, v, band, p["lam_qk"], p["sub_g"], lam_init)
        x = _merge_call(x, ya, yb, gates, p["w_br_a"], p["w_br_b"], p["w_out"])
        x = _ffn_call(x, p["g_ffn"], p["w_up"], p["conv_w"], p["conv_b"], p["w_down"],
                      g_final, final=(l == len(layers) - 1))
    return x


def kernel(x_prompt, x_sample, rel_bias, g_mix, w_in, sgu_g, sgu_w, sgu_b, lam_qk, sub_g,
           w_br, w_out, g_ffn, w_up, conv_w, conv_b, w_down, g_final):
    band = _bias_band(rel_bias)
    layers = [_prep_layer(l, g_mix, w_in, sgu_g, sgu_w, sgu_b, lam_qk, sub_g, w_br, w_out,
                          g_ffn, w_up, conv_w, conv_b, w_down) for l in range(DEPTH)]
    g_fin = g_final[None]
    return (_trunk(x_prompt, band, layers, g_fin), _trunk(x_sample, band, layers, g_fin))
```

```python
import functools
import math

import jax
import jax.numpy as jnp
from jax import lax
from jax.experimental import pallas as pl
from jax.experimental.pallas import tpu as pltpu

D_MODEL = 1024
DEPTH = 2
CHUNK = 128
A_GROUPS = 8
A_WIDTH = D_MODEL
B_HEADS = 8
B_HEAD_DIM = D_MODEL // (2 * B_HEADS)
HEAD_W = 2 * B_HEAD_DIM
B_WIDTH = B_HEADS * HEAD_W
ATTN_SCALE = B_HEAD_DIM ** -0.5
N_BUCKETS = 32
MAX_DIST = 128
D_FF = 2816
CONV_W = 3
EPS = 1e-6
IN_COLS = 2 * A_WIDTH + 3 * B_WIDTH + 2 * D_MODEL

V7X_LANES = 128
V7X_SUBLANES = 8
V7X_MXU_DIM = 256
V7X_VMEM_BYTES = 64 * 1024 * 1024

TM_PROJ = 512
TM_MERGE = 512
TM_FFN = 512
HALO = V7X_SUBLANES
FC = V7X_MXU_DIM
N_FC = D_FF // FC
FFN_RB = TM_FFN
FFN_SLOTS = 2
TQ = V7X_MXU_DIM
CK = V7X_MXU_DIM
ATTN_RB = CK
ATTN_SLOTS = 3
N_BAND = 5
VT_ROWS = HEAD_W + 2 * V7X_SUBLANES
LOG2E = math.log2(math.e)
GELU_C0 = math.sqrt(2.0 / math.pi)
GELU_C1 = 0.044715 * GELU_C0

assert D_FF % FC == 0
assert MAX_DIST <= CK and MAX_DIST <= TQ


def _vmem_limit(nbytes):
    return int(min(nbytes, V7X_VMEM_BYTES - 4 * 1024 * 1024))


def _rms(x, g):
    return x * lax.rsqrt(jnp.mean(x * x, axis=-1, keepdims=True) + EPS) * g


def _run_interleaved(*stages):
    order = sorted(((i + 0.5) / len(pieces), si, i)
                   for si, (pieces, _) in enumerate(stages) for i in range(len(pieces)))
    for _, si, i in order:
        stages[si][0][i]()
    return [result() for _, result in stages]


def _skewed_pipeline(n, stage_a, stage_b, stage_c, slots=2, unroll=False):
    assert n >= 2 and slots >= 2
    run = _run_interleaved

    def step(t, t_static, ra, rb):
        ra_new, rb_new, _ = run(stage_a(t, t_static % slots),
                                stage_b(t - 1, (t_static - 1) % slots, ra),
                                stage_c(t - 2, (t_static - 2) % slots, rb))
        return ra_new, rb_new

    ra0, = run(stage_a(0, 0))
    ra, rb = run(stage_a(1, 1), stage_b(0, 0, ra0))
    t0 = 2
    while (n - t0) % slots:
        ra, rb = step(t0, t0, ra, rb)
        t0 += 1

    def body(k, carry):
        for j in range(slots):
            carry = step(t0 + slots * k + j, t0 + j, *carry)
        return carry

    if unroll:
        for k in range((n - t0) // slots):
            ra, rb = body(k, (ra, rb))
    else:
        ra, rb = lax.fori_loop(0, (n - t0) // slots, body, (ra, rb))
    rb_last, _ = run(stage_b(n - 1, (n - 1) % slots, ra), stage_c(n - 2, (n - 2) % slots, rb))
    run(stage_c(n - 1, (n - 1) % slots, rb_last))


def _rel_bucket(rel):
    nb = N_BUCKETS // 2
    max_exact = nb // 2
    ret = jnp.where(rel > 0, nb, 0)
    n = jnp.abs(rel)
    nf = jnp.maximum(n, 1).astype(jnp.float32)
    large = max_exact + (jnp.log(nf / max_exact) / math.log(MAX_DIST / max_exact)
                         * (nb - max_exact)).astype(jnp.int32)
    large = jnp.minimum(large, nb - 1)
    return ret + jnp.where(n < max_exact, n, large)


def _band_kernel(table_ref, idx_ref, out_ref):
    h = pl.program_id(0)
    idx = idx_ref[...]
    acc = jnp.zeros(idx.shape, jnp.float32)
    for b in range(N_BUCKETS):
        acc = jnp.where(idx == b, table_ref[b, h] * LOG2E, acc)
    out_ref[0] = acc


def _bias_band(rel_bias):
    d = jnp.arange(N_BAND, dtype=jnp.int32)[:, None, None] - N_BAND // 2
    kk = jnp.arange(CK, dtype=jnp.int32)[None, :, None]
    qq = jnp.arange(TQ, dtype=jnp.int32)[None, None, :]
    idx = _rel_bucket(d * CK + kk - qq)
    return pl.pallas_call(
        _band_kernel,
        out_shape=jax.ShapeDtypeStruct((B_HEADS, N_BAND, CK, TQ), jnp.float32),
        grid=(B_HEADS,),
        in_specs=[
            pl.BlockSpec(memory_space=pltpu.SMEM),
            pl.BlockSpec((N_BAND, CK, TQ), lambda h: (0, 0, 0)),
        ],
        out_specs=pl.BlockSpec((1, N_BAND, CK, TQ), lambda h: (h, 0, 0, 0)),
        name="bias_band",
    )(rel_bias, idx)


def _proj_kernel(x_ref, g_ref, w_ref, sg_ref, ws_ref, bs_ref,
                 ya_ref, q_ref, k_ref, v_ref, gate_ref):
    tm = x_ref.shape[0]
    h = _rms(x_ref[...], g_ref[...]).astype(jnp.bfloat16)

    def proj(lo, hi):
        return jnp.dot(h, w_ref[:, lo:hi], preferred_element_type=jnp.float32)

    u = proj(0, A_WIDTH)
    vn = _rms(proj(A_WIDTH, 2 * A_WIDTH), sg_ref[...]).astype(jnp.bfloat16)
    for n in range(tm // CHUNK):
        rows = slice(n * CHUNK, (n + 1) * CHUNK)
        for g in range(A_GROUPS):
            cols = slice(g * CHUNK, (g + 1) * CHUNK)
            mixed = jnp.dot(ws_ref[g], vn[rows, cols], preferred_element_type=jnp.float32)
            ya_ref[rows, cols] = (u[rows, cols] * (mixed + bs_ref[:, cols])).astype(ya_ref.dtype)

    base = 2 * A_WIDTH
    for i, o_ref in enumerate((q_ref, k_ref, v_ref)):
        r = proj(base + i * B_WIDTH, base + (i + 1) * B_WIDTH)
        for hd in range(B_HEADS):
            o_ref[0, hd] = r[:, hd * HEAD_W:(hd + 1) * HEAD_W].astype(o_ref.dtype)
    base += 3 * B_WIDTH
    gate_ref[...] = proj(base, base + 2 * D_MODEL).astype(gate_ref.dtype)


def _proj_call(x, g_mix, w_in, sgu_g, sgu_w, sgu_bfull):
    bsz, s, _ = x.shape
    m = bsz * s
    tm = TM_PROJ
    tiles_per_seq = s // tm
    const = lambda shape: pl.BlockSpec(shape, lambda i: (0,) * len(shape),
                                       pipeline_mode=pl.Buffered(1))
    head_spec = pl.BlockSpec((1, B_HEADS, tm, HEAD_W),
                             lambda i: (i // tiles_per_seq, 0, i % tiles_per_seq, 0))
    head_shape = jax.ShapeDtypeStruct((bsz, B_HEADS, s, HEAD_W), jnp.bfloat16)
    return pl.pallas_call(
        _proj_kernel,
        out_shape=(jax.ShapeDtypeStruct((m, A_WIDTH), jnp.bfloat16),
                   head_shape, head_shape, head_shape,
                   jax.ShapeDtypeStruct((m, 2 * D_MODEL), jnp.bfloat16)),
        grid=(m // tm,),
        in_specs=[
            pl.BlockSpec((tm, D_MODEL), lambda i: (i, 0)),
            const((1, D_MODEL)),
            const((D_MODEL, IN_COLS)),
            const((1, A_WIDTH)),
            const((A_GROUPS, CHUNK, CHUNK)),
            const((CHUNK, A_WIDTH)),
        ],
        out_specs=(pl.BlockSpec((tm, A_WIDTH), lambda i: (i, 0)),
                   head_spec, head_spec, head_spec,
                   pl.BlockSpec((tm, 2 * D_MODEL), lambda i: (i, 0))),
        compiler_params=pltpu.CompilerParams(
            dimension_semantics=("parallel",),
            vmem_limit_bytes=_vmem_limit(56 * 1024 * 1024)),
        name="proj_sgu",
    )(x.reshape(m, D_MODEL), g_mix, w_in, sgu_g, sgu_w, sgu_bfull)


def _attn_kernel(lam_init, q_ref, k_ref, v_ref, band_ref, lq_ref, subg_ref, o_ref,
                 vt_ref, *slot_refs):
    l_refs, p_refs = slot_refs[:ATTN_SLOTS], slot_refs[ATTN_SLOTS:]
    s = q_ref.shape[2]
    nq, nk = s // TQ, s // CK
    rg = ATTN_RB // V7X_SUBLANES

    lq = lq_ref[...]
    lam = (jnp.exp(jnp.sum(lq[0:1] * lq[1:2], axis=-1, keepdims=True))
           - jnp.exp(jnp.sum(lq[2:3] * lq[3:4], axis=-1, keepdims=True)) + lam_init)

    for c in range(nk):
        cols = slice(c * CK, (c + 1) * CK)
        vt_ref[:HEAD_W, cols] = v_ref[0, 0, cols, :].astype(jnp.float32).T.astype(vt_ref.dtype)
    pad_row = lax.broadcasted_iota(jnp.int32, (VT_ROWS - HEAD_W, s), 0)
    vt_ref[HEAD_W:, :] = jnp.where(pad_row == 0, 1.0, 0.0).astype(vt_ref.dtype)

    lane = lax.broadcasted_iota(jnp.int32, (TQ, HEAD_W), 1)
    nt_dims = (((1,), (1,)), ((), ()))

    def q_rows(qi):
        return pl.ds(qi * TQ, TQ) if isinstance(qi, int) else pl.ds(pl.multiple_of(qi * TQ, TQ), TQ)

    def band_index(c, qi):
        lo, hi = -(N_BAND // 2), N_BAND // 2
        if isinstance(qi, int):
            return min(max(c - qi, lo), hi) - lo
        return jnp.clip(c - qi, lo, hi) - lo

    def logit_stage(qi, slot):
        l_ref = l_refs[slot]
        qm = []
        mx = [None, None]

        def piece(m, c):
            def run():
                if not qm:
                    q = q_ref[0, 0, q_rows(qi), :].astype(jnp.float32)
                    qm.append(jnp.where(lane < B_HEAD_DIM, q, 0.0).astype(jnp.bfloat16))
                    qm.append(jnp.where(lane >= B_HEAD_DIM, q, 0.0).astype(jnp.bfloat16))
                logit = lax.dot_general(k_ref[0, 0, c * CK:(c + 1) * CK, :], qm[m], nt_dims,
                                        preferred_element_type=jnp.float32)
                d = band_index(c, qi)
                for r in range(0, CK, ATTN_RB):
                    rows = slice(r, r + ATTN_RB)
                    lc = logit[rows] + band_ref[0, d, rows, :]
                    l_ref[m, c, rows, :] = lc
                    cm = jnp.max(lc.reshape(rg, V7X_SUBLANES, TQ), axis=0)
                    mx[m] = cm if mx[m] is None else jnp.maximum(mx[m], cm)
            return run

        pieces = [piece(m, c) for c in range(nk) for m in range(2)]
        return pieces, lambda: tuple(jnp.max(v, axis=0, keepdims=True) for v in mx)

    def exp_stage(qi, slot, mrow):
        l_ref, p_ref = l_refs[slot], p_refs[slot]

        def piece(m, r):
            def run():
                x = l_ref[m, r // CK, r % CK:r % CK + ATTN_RB, :] - mrow[m]
                p_ref[m, r:r + ATTN_RB, :] = jnp.exp2(x.astype(p_ref.dtype))
            return run

        return [piece(m, r) for m in range(2) for r in range(0, s, ATTN_RB)], lambda: ()

    def value_stage(qi, slot, _):
        acc = []

        def matmul(m):
            def run():
                acc.append(jnp.dot(vt_ref[...], p_refs[slot][m],
                                   preferred_element_type=jnp.float32))
            return run

        def finish():
            sums = [a[HEAD_W:HEAD_W + 1] for a in acc]
            o = acc[0][:HEAD_W] / sums[0] - acc[1][:HEAD_W] * (lam / sums[1])
            o = o * lax.rsqrt(jnp.mean(o * o, axis=0, keepdims=True) + EPS)
            o = o.T * subg_ref[...] * (1.0 - lam_init)
            o_ref[0, 0, q_rows(qi), :] = o.astype(o_ref.dtype)

        return [matmul(0), matmul(1), finish], lambda: ()

    _skewed_pipeline(nq, logit_stage, exp_stage, value_stage, slots=ATTN_SLOTS)


def _attn_call(q, k, v, band, lam_qk, sub_g, lam_init):
    bsz, _, s, _ = q.shape
    head = pl.BlockSpec((1, 1, s, HEAD_W), lambda b, h: (b, h, 0, 0))
    return pl.pallas_call(
        functools.partial(_attn_kernel, lam_init),
        out_shape=jax.ShapeDtypeStruct(q.shape, jnp.bfloat16),
        grid=(bsz, B_HEADS),
        in_specs=[
            head, head, head,
            pl.BlockSpec((1, N_BAND, CK, TQ), lambda b, h: (h, 0, 0, 0)),
            pl.BlockSpec((4, B_HEAD_DIM), lambda b, h: (0, 0)),
            pl.BlockSpec((1, HEAD_W), lambda b, h: (0, 0)),
        ],
        out_specs=head,
        scratch_shapes=[
            pltpu.VMEM((VT_ROWS, s), jnp.bfloat16),
            *[pltpu.VMEM((2, s // CK, CK, TQ), jnp.float32)] * ATTN_SLOTS,
            *[pltpu.VMEM((2, s, TQ), jnp.bfloat16)] * ATTN_SLOTS,
        ],
        compiler_params=pltpu.CompilerParams(
            dimension_semantics=("parallel", "parallel"),
            vmem_limit_bytes=_vmem_limit(48 * 1024 * 1024)),
        name="diff_attn",
    )(q, k, v, band, lam_qk, sub_g)


def _merge_kernel(x_ref, ya_ref, yb_ref, gate_ref, wa_ref, wb_ref, wo_ref, o_ref):
    yb = jnp.concatenate([yb_ref[0, hd] for hd in range(B_HEADS)], axis=-1)
    br_a = jnp.dot(ya_ref[...], wa_ref[...], preferred_element_type=jnp.float32)
    br_b = jnp.dot(yb, wb_ref[...], preferred_element_type=jnp.float32)
    gate = gate_ref[...].astype(jnp.float32)
    merged = (jax.nn.sigmoid(gate[:, :D_MODEL]) * br_a
              + jax.nn.sigmoid(gate[:, D_MODEL:]) * br_b)
    o_ref[...] = x_ref[...] + jnp.dot(merged.astype(jnp.bfloat16), wo_ref[...],
                                      preferred_element_type=jnp.float32)


def _merge_call(x, ya, yb, gates, w_br_a, w_br_b, w_out):
    bsz, s, _ = x.shape
    m = bsz * s
    tm = TM_MERGE
    tiles_per_seq = s // tm
    row = lambda width: pl.BlockSpec((tm, width), lambda i: (i, 0))
    const = lambda shape: pl.BlockSpec(shape, lambda i: (0,) * len(shape),
                                       pipeline_mode=pl.Buffered(1))
    out = pl.pallas_call(
        _merge_kernel,
        out_shape=jax.ShapeDtypeStruct((m, D_MODEL), jnp.float32),
        grid=(m // tm,),
        in_specs=[
            row(D_MODEL), row(A_WIDTH),
            pl.BlockSpec((1, B_HEADS, tm, HEAD_W),
                         lambda i: (i // tiles_per_seq, 0, i % tiles_per_seq, 0)),
            row(2 * D_MODEL),
            const((A_WIDTH, D_MODEL)), const((B_WIDTH, D_MODEL)), const((D_MODEL, D_MODEL)),
        ],
        out_specs=row(D_MODEL),
        compiler_params=pltpu.CompilerParams(
            dimension_semantics=("parallel",),
            vmem_limit_bytes=_vmem_limit(48 * 1024 * 1024)),
        name="merge_out",
    )(x.reshape(m, D_MODEL), ya, yb, gates, w_br_a, w_br_b, w_out)
    return out.reshape(bsz, s, D_MODEL)


def _ffn_kernel(final, x_ref, xp_ref, xn_ref, g_ref, wup_ref, cw_ref, cb_ref, wdn_ref, gf_ref,
                o_ref, h_ref, *slot_refs):
    up_refs, gated_refs = slot_refs[:FFN_SLOTS], slot_refs[FFN_SLOTS:]
    tm = x_ref.shape[1]
    j = pl.program_id(1)
    x = x_ref[0]
    keep_p = (j > 0).astype(jnp.float32)
    keep_n = (j < pl.num_programs(1) - 1).astype(jnp.float32)
    g = g_ref[...]
    h_ref[...] = jnp.concatenate(
        [_rms(xp_ref[0], g) * keep_p, _rms(x, g), _rms(xn_ref[0], g) * keep_n],
        axis=0).astype(h_ref.dtype)
    o_ref[0] = x

    nothing = lambda: ()

    def ff_cols(half, c):
        return slice(half * D_FF + c * FC, half * D_FF + (c + 1) * FC)

    def up_stage(c, slot):
        def piece(half):
            def run():
                up_refs[slot][half] = jnp.dot(h_ref[...], wup_ref[:, ff_cols(half, c)],
                                              preferred_element_type=jnp.float32)
            return run
        return [piece(0), piece(1)], nothing

    def conv(up_ref, half, c, r0):
        n = FFN_RB + 2 * HALO
        ext = up_ref[half, r0:r0 + n, :]
        w = cw_ref[:, ff_cols(half, c)]
        prev = pltpu.roll(ext, 1, 0)
        nxt = pltpu.roll(ext, n - 1, 0)
        mid = slice(HALO, HALO + FFN_RB)
        return (prev[mid] * w[0:1] + ext[mid] * w[1:2] + nxt[mid] * w[2:3]
                + cb_ref[:, ff_cols(half, c)])

    def gate_stage(c, slot, _):
        def piece(r0):
            def run():
                a = conv(up_refs[slot], 0, c, r0)
                half_b = conv(up_refs[slot], 1, c, r0)
                t = jnp.tanh(a * (GELU_C0 + GELU_C1 * (a * a)))
                gated_refs[slot][r0:r0 + FFN_RB, :] = ((a + a * t) * half_b).astype(jnp.bfloat16)
            return run
        return [piece(r0) for r0 in range(0, tm, FFN_RB)], nothing

    def down_stage(c, slot, _):
        def piece(n0):
            def run():
                cols = slice(n0, n0 + V7X_MXU_DIM)
                o_ref[0, :, cols] += jnp.dot(gated_refs[slot][...],
                                             wdn_ref[c * FC:(c + 1) * FC, cols],
                                             preferred_element_type=jnp.float32)
            return run
        return [piece(n0) for n0 in range(0, D_MODEL, V7X_MXU_DIM)], nothing

    _skewed_pipeline(N_FC, up_stage, gate_stage, down_stage, slots=FFN_SLOTS, unroll=True)
    if final:
        o_ref[0] = _rms(o_ref[0], gf_ref[...])


def _ffn_call(x, g_ffn, w_up, conv_w, conv_b, w_down, g_final, final):
    bsz, s, _ = x.shape
    tm = TM_FFN
    nt = s // tm
    hb = tm // HALO
    const = lambda shape: pl.BlockSpec(shape, lambda b, j: (0,) * len(shape),
                                       pipeline_mode=pl.Buffered(1))
    return pl.pallas_call(
        functools.partial(_ffn_kernel, final),
        out_shape=jax.ShapeDtypeStruct(x.shape, jnp.float32),
        grid=(bsz, nt),
        in_specs=[
            pl.BlockSpec((1, tm, D_MODEL), lambda b, j: (b, j, 0)),
            pl.BlockSpec((1, HALO, D_MODEL), lambda b, j: (b, jnp.maximum(j * hb - 1, 0), 0)),
            pl.BlockSpec((1, HALO, D_MODEL),
                         lambda b, j: (b, jnp.minimum((j + 1) * hb, nt * hb - 1), 0)),
            const((1, D_MODEL)),
            const((D_MODEL, 2 * D_FF)),
            const((CONV_W, 2 * D_FF)),
            const((1, 2 * D_FF)),
            const((D_FF, D_MODEL)),
            const((1, D_MODEL)),
        ],
        out_specs=pl.BlockSpec((1, tm, D_MODEL), lambda b, j: (b, j, 0)),
        scratch_shapes=[
            pltpu.VMEM((tm + 2 * HALO, D_MODEL), jnp.bfloat16),
            *[pltpu.VMEM((2, tm + 2 * HALO, FC), jnp.float32)] * FFN_SLOTS,
            *[pltpu.VMEM((tm, FC), jnp.bfloat16)] * FFN_SLOTS,
        ],
        compiler_params=pltpu.CompilerParams(
            dimension_semantics=("parallel", "parallel"),
            vmem_limit_bytes=_vmem_limit(56 * 1024 * 1024)),
        name="conv_ffn",
    )(x, x, x, g_ffn, w_up, conv_w, conv_b, w_down, g_final)


def _prep_layer(l, g_mix, w_in, sgu_g, sgu_w, sgu_b, lam_qk, sub_g, w_br, w_out,
                g_ffn, w_up, conv_w, conv_b, w_down):
    bf = jnp.bfloat16
    q_lo, q_hi = 2 * A_WIDTH, 2 * A_WIDTH + B_WIDTH
    col_scale = jnp.ones((IN_COLS,), jnp.float32).at[q_lo:q_hi].set(ATTN_SCALE * LOG2E)
    gate_scale = jnp.ones((2 * D_FF,), jnp.float32).at[D_FF:].set(0.5)
    return dict(
        g_mix=g_mix[l][None],
        w_in=(w_in[l] * col_scale).astype(bf),
        sgu_g=sgu_g[l][None],
        sgu_w=sgu_w[l].astype(bf),
        sgu_bfull=jnp.repeat(sgu_b[l].T, CHUNK, axis=1),
        lam_qk=lam_qk[l],
        sub_g=sub_g[l][None],
        w_br_a=w_br[l, 0].astype(bf),
        w_br_b=w_br[l, 1].astype(bf),
        w_out=w_out[l].astype(bf),
        g_ffn=g_ffn[l][None],
        w_up=w_up[l].astype(bf),
        conv_w=conv_w[l] * gate_scale,
        conv_b=(conv_b[l] * gate_scale)[None],
        w_down=w_down[l].astype(bf),
    )


def _trunk(x, band, layers, g_final):
    for l, p in enumerate(layers):
        lam_init = 0.8 - 0.6 * math.exp(-0.3 * l)
        ya, q, k, v, gates = _proj_call(x, p["g_mix"], p["w_in"], p["sgu_g"], p["sgu_w"],
                                        p["sgu_bfull"])
        yb = _attn_call(q, k, v, band, p["lam_qk"], p["sub_g"], lam_init)
        x = _merge_call(x, ya, yb, gates, p["w_br_a"], p["w_br_b"], p["w_out"])
        x = _ffn_call(x, p["g_ffn"], p["w_up"], p["conv_w"], p["conv_b"], p["w_down"],
                      g_final, final=(l == len(layers) - 1))
    return x


def kernel(x_prompt, x_sample, rel_bias, g_mix, w_in, sgu_g, sgu_w, sgu_b, lam_qk, sub_g,
           w_br, w_out, g_ffn, w_up, conv_w, conv_b, w_down, g_final):
    band = _bias_band(rel_bias)
    layers = [_prep_layer(l, g_mix, w_in, sgu_g, sgu_w, sgu_b, lam_qk, sub_g, w_br, w_out,
                          g_ffn, w_up, conv_w, conv_b, w_down) for l in range(DEPTH)]
    g_fin = g_final[None]
    return (_trunk(x_prompt, band, layers, g_fin), _trunk(x_sample, band, layers, g_fin))
```

```python
import functools
import math

import jax
import jax.numpy as jnp
from jax import lax
from jax.experimental import pallas as pl
from jax.experimental.pallas import tpu as pltpu

D_MODEL = 1024
DEPTH = 2
CHUNK = 128
A_GROUPS = 8
A_WIDTH = D_MODEL
B_HEADS = 8
B_HEAD_DIM = D_MODEL // (2 * B_HEADS)
HEAD_W = 2 * B_HEAD_DIM
B_WIDTH = B_HEADS * HEAD_W
ATTN_SCALE = B_HEAD_DIM ** -0.5
N_BUCKETS = 32
MAX_DIST = 128
D_FF = 2816
CONV_W = 3
EPS = 1e-6
IN_COLS = 2 * A_WIDTH + 3 * B_WIDTH + 2 * D_MODEL

V7X_LANES = 128
V7X_SUBLANES = 8
V7X_MXU_DIM = 256
V7X_VMEM_BYTES = 64 * 1024 * 1024

TM_PROJ = 512
TM_MERGE = 512
TM_FFN = 512
HALO = V7X_SUBLANES
FC = V7X_MXU_DIM
N_FC = D_FF // FC
FFN_RB = TM_FFN
FFN_SLOTS = 2
TQ = V7X_MXU_DIM
CK = V7X_MXU_DIM
ATTN_RB = 64
ATTN_SLOTS = 3
N_BAND = 5
VT_ROWS = HEAD_W + 2 * V7X_SUBLANES
LOG2E = math.log2(math.e)
GELU_C0 = math.sqrt(2.0 / math.pi)
GELU_C1 = 0.044715 * GELU_C0

assert D_FF % FC == 0
assert MAX_DIST <= CK and MAX_DIST <= TQ


def _vmem_limit(nbytes):
    return int(min(nbytes, V7X_VMEM_BYTES - 4 * 1024 * 1024))


def _rms(x, g):
    return x * lax.rsqrt(jnp.mean(x * x, axis=-1, keepdims=True) + EPS) * g


def _run_interleaved(*stages):
    order = sorted(((i + 0.5) / len(pieces), si, i)
                   for si, (pieces, _) in enumerate(stages) for i in range(len(pieces)))
    for _, si, i in order:
        stages[si][0][i]()
    return [result() for _, result in stages]


def _skewed_pipeline(n, stage_a, stage_b, stage_c, slots=2, unroll=False):
    assert n >= 2 and slots >= 2
    run = _run_interleaved

    def step(t, t_static, ra, rb):
        ra_new, rb_new, _ = run(stage_a(t, t_static % slots),
                                stage_b(t - 1, (t_static - 1) % slots, ra),
                                stage_c(t - 2, (t_static - 2) % slots, rb))
        return ra_new, rb_new

    ra0, = run(stage_a(0, 0))
    ra, rb = run(stage_a(1, 1), stage_b(0, 0, ra0))
    t0 = 2
    while (n - t0) % slots:
        ra, rb = step(t0, t0, ra, rb)
        t0 += 1

    def body(k, carry):
        for j in range(slots):
            carry = step(t0 + slots * k + j, t0 + j, *carry)
        return carry

    if unroll:
        for k in range((n - t0) // slots):
            ra, rb = body(k, (ra, rb))
    else:
        ra, rb = lax.fori_loop(0, (n - t0) // slots, body, (ra, rb))
    rb_last, _ = run(stage_b(n - 1, (n - 1) % slots, ra), stage_c(n - 2, (n - 2) % slots, rb))
    run(stage_c(n - 1, (n - 1) % slots, rb_last))


def _rel_bucket(rel):
    nb = N_BUCKETS // 2
    max_exact = nb // 2
    ret = jnp.where(rel > 0, nb, 0)
    n = jnp.abs(rel)
    nf = jnp.maximum(n, 1).astype(jnp.float32)
    large = max_exact + (jnp.log(nf / max_exact) / math.log(MAX_DIST / max_exact)
                         * (nb - max_exact)).astype(jnp.int32)
    large = jnp.minimum(large, nb - 1)
    return ret + jnp.where(n < max_exact, n, large)


def _band_kernel(table_ref, idx_ref, out_ref):
    h = pl.program_id(0)
    idx = idx_ref[...]
    acc = jnp.zeros(idx.shape, jnp.float32)
    for b in range(N_BUCKETS):
        acc = jnp.where(idx == b, table_ref[b, h] * LOG2E, acc)
    out_ref[0] = acc


def _bias_band(rel_bias):
    d = jnp.arange(N_BAND, dtype=jnp.int32)[:, None, None] - N_BAND // 2
    kk = jnp.arange(CK, dtype=jnp.int32)[None, :, None]
    qq = jnp.arange(TQ, dtype=jnp.int32)[None, None, :]
    idx = _rel_bucket(d * CK + kk - qq)
    return pl.pallas_call(
        _band_kernel,
        out_shape=jax.ShapeDtypeStruct((B_HEADS, N_BAND, CK, TQ), jnp.float32),
        grid=(B_HEADS,),
        in_specs=[
            pl.BlockSpec(memory_space=pltpu.SMEM),
            pl.BlockSpec((N_BAND, CK, TQ), lambda h: (0, 0, 0)),
        ],
        out_specs=pl.BlockSpec((1, N_BAND, CK, TQ), lambda h: (h, 0, 0, 0)),
        name="bias_band",
    )(rel_bias, idx)


def _proj_kernel(x_ref, g_ref, w_ref, sg_ref, ws_ref, bs_ref,
                 ya_ref, q_ref, k_ref, v_ref, gate_ref):
    tm = x_ref.shape[0]
    h = _rms(x_ref[...], g_ref[...]).astype(jnp.bfloat16)

    def proj(lo, hi):
        return jnp.dot(h, w_ref[:, lo:hi], preferred_element_type=jnp.float32)

    u = proj(0, A_WIDTH)
    vn = _rms(proj(A_WIDTH, 2 * A_WIDTH), sg_ref[...]).astype(jnp.bfloat16)
    for n in range(tm // CHUNK):
        rows = slice(n * CHUNK, (n + 1) * CHUNK)
        for g in range(A_GROUPS):
            cols = slice(g * CHUNK, (g + 1) * CHUNK)
            mixed = jnp.dot(ws_ref[g], vn[rows, cols], preferred_element_type=jnp.float32)
            ya_ref[rows, cols] = (u[rows, cols] * (mixed + bs_ref[:, cols])).astype(ya_ref.dtype)

    base = 2 * A_WIDTH
    for i, o_ref in enumerate((q_ref, k_ref, v_ref)):
        r = proj(base + i * B_WIDTH, base + (i + 1) * B_WIDTH)
        for hd in range(B_HEADS):
            o_ref[0, hd] = r[:, hd * HEAD_W:(hd + 1) * HEAD_W].astype(o_ref.dtype)
    base += 3 * B_WIDTH
    gate_ref[...] = proj(base, base + 2 * D_MODEL).astype(gate_ref.dtype)


def _proj_call(x, g_mix, w_in, sgu_g, sgu_w, sgu_bfull):
    bsz, s, _ = x.shape
    m = bsz * s
    tm = TM_PROJ
    tiles_per_seq = s // tm
    const = lambda shape: pl.BlockSpec(shape, lambda i: (0,) * len(shape),
                                       pipeline_mode=pl.Buffered(1))
    head_spec = pl.BlockSpec((1, B_HEADS, tm, HEAD_W),
                             lambda i: (i // tiles_per_seq, 0, i % tiles_per_seq, 0))
    head_shape = jax.ShapeDtypeStruct((bsz, B_HEADS, s, HEAD_W), jnp.bfloat16)
    return pl.pallas_call(
        _proj_kernel,
        out_shape=(jax.ShapeDtypeStruct((m, A_WIDTH), jnp.bfloat16),
                   head_shape, head_shape, head_shape,
                   jax.ShapeDtypeStruct((m, 2 * D_MODEL), jnp.bfloat16)),
        grid=(m // tm,),
        in_specs=[
            pl.BlockSpec((tm, D_MODEL), lambda i: (i, 0)),
            const((1, D_MODEL)),
            const((D_MODEL, IN_COLS)),
            const((1, A_WIDTH)),
            const((A_GROUPS, CHUNK, CHUNK)),
            const((CHUNK, A_WIDTH)),
        ],
        out_specs=(pl.BlockSpec((tm, A_WIDTH), lambda i: (i, 0)),
                   head_spec, head_spec, head_spec,
                   pl.BlockSpec((tm, 2 * D_MODEL), lambda i: (i, 0))),
        compiler_params=pltpu.CompilerParams(
            dimension_semantics=("parallel",),
            vmem_limit_bytes=_vmem_limit(56 * 1024 * 1024)),
        name="proj_sgu",
    )(x.reshape(m, D_MODEL), g_mix, w_in, sgu_g, sgu_w, sgu_bfull)


def _attn_kernel(lam_init, q_ref, k_ref, v_ref, band_ref, lq_ref, subg_ref, o_ref,
                 vt_ref, *slot_refs):
    l_refs, p_refs = slot_refs[:ATTN_SLOTS], slot_refs[ATTN_SLOTS:]
    s = q_ref.shape[2]
    nq, nk = s // TQ, s // CK
    rg = ATTN_RB // V7X_SUBLANES

    lq = lq_ref[...]
    lam = (jnp.exp(jnp.sum(lq[0:1] * lq[1:2], axis=-1, keepdims=True))
           - jnp.exp(jnp.sum(lq[2:3] * lq[3:4], axis=-1, keepdims=True)) + lam_init)

    for c in range(nk):
        cols = slice(c * CK, (c + 1) * CK)
        vt_ref[:HEAD_W, cols] = v_ref[0, 0, cols, :].astype(jnp.float32).T.astype(vt_ref.dtype)
    pad_row = lax.broadcasted_iota(jnp.int32, (VT_ROWS - HEAD_W, s), 0)
    vt_ref[HEAD_W:, :] = jnp.where(pad_row == 0, 1.0, 0.0).astype(vt_ref.dtype)

    lane = lax.broadcasted_iota(jnp.int32, (TQ, HEAD_W), 1)
    nt_dims = (((1,), (1,)), ((), ()))

    def q_rows(qi):
        return pl.ds(qi * TQ, TQ) if isinstance(qi, int) else pl.ds(pl.multiple_of(qi * TQ, TQ), TQ)

    def band_index(c, qi):
        lo, hi = -(N_BAND // 2), N_BAND // 2
        if isinstance(qi, int):
            return min(max(c - qi, lo), hi) - lo
        return jnp.clip(c - qi, lo, hi) - lo

    def logit_stage(qi, slot):
        l_ref = l_refs[slot]
        qm = []
        mx = [None, None]

        def piece(m, c):
            def run():
                if not qm:
                    q = q_ref[0, 0, q_rows(qi), :].astype(jnp.float32)
                    qm.append(jnp.where(lane < B_HEAD_DIM, q, 0.0).astype(jnp.bfloat16))
                    qm.append(jnp.where(lane >= B_HEAD_DIM, q, 0.0).astype(jnp.bfloat16))
                logit = lax.dot_general(k_ref[0, 0, c * CK:(c + 1) * CK, :], qm[m], nt_dims,
                                        preferred_element_type=jnp.float32)
                d = band_index(c, qi)
                for r in range(0, CK, ATTN_RB):
                    rows = slice(r, r + ATTN_RB)
                    lc = logit[rows] + band_ref[0, d, rows, :]
                    l_ref[m, c, rows, :] = lc
                    cm = jnp.max(lc.reshape(rg, V7X_SUBLANES, TQ), axis=0)
                    mx[m] = cm if mx[m] is None else jnp.maximum(mx[m], cm)
            return run

        pieces = [piece(m, c) for c in range(nk) for m in range(2)]
        return pieces, lambda: tuple(jnp.max(v, axis=0, keepdims=True) for v in mx)

    def exp_stage(qi, slot, mrow):
        l_ref, p_ref = l_refs[slot], p_refs[slot]

        def piece(m, r):
            def run():
                x = l_ref[m, r // CK, r % CK:r % CK + ATTN_RB, :] - mrow[m]
                p_ref[m, r:r + ATTN_RB, :] = jnp.exp2(x.astype(p_ref.dtype))
            return run

        return [piece(m, r) for m in range(2) for r in range(0, s, ATTN_RB)], lambda: ()

    def value_stage(qi, slot, _):
        acc = []

        def matmul(m):
            def run():
                acc.append(jnp.dot(vt_ref[...], p_refs[slot][m],
                                   preferred_element_type=jnp.float32))
            return run

        def finish():
            sums = [a[HEAD_W:HEAD_W + 1] for a in acc]
            o = acc[0][:HEAD_W] / sums[0] - acc[1][:HEAD_W] * (lam / sums[1])
            o = o * lax.rsqrt(jnp.mean(o * o, axis=0, keepdims=True) + EPS)
            o = o.T * subg_ref[...] * (1.0 - lam_init)
            o_ref[0, 0, q_rows(qi), :] = o.astype(o_ref.dtype)

        return [matmul(0), matmul(1), finish], lambda: ()

    _skewed_pipeline(nq, logit_stage, exp_stage, value_stage, slots=ATTN_SLOTS)


def _attn_call(q, k, v, band, lam_qk, sub_g, lam_init):
    bsz, _, s, _ = q.shape
    head = pl.BlockSpec((1, 1, s, HEAD_W), lambda b, h: (b, h, 0, 0))
    return pl.pallas_call(
        functools.partial(_attn_kernel, lam_init),
        out_shape=jax.ShapeDtypeStruct(q.shape, jnp.bfloat16),
        grid=(bsz, B_HEADS),
        in_specs=[
            head, head, head,
            pl.BlockSpec((1, N_BAND, CK, TQ), lambda b, h: (h, 0, 0, 0)),
            pl.BlockSpec((4, B_HEAD_DIM), lambda b, h: (0, 0)),
            pl.BlockSpec((1, HEAD_W), lambda b, h: (0, 0)),
        ],
        out_specs=head,
        scratch_shapes=[
            pltpu.VMEM((VT_ROWS, s), jnp.bfloat16),
            *[pltpu.VMEM((2, s // CK, CK, TQ), jnp.float32)] * ATTN_SLOTS,
            *[pltpu.VMEM((2, s, TQ), jnp.bfloat16)] * ATTN_SLOTS,
        ],
        compiler_params=pltpu.CompilerParams(
            dimension_semantics=("parallel", "parallel"),
            vmem_limit_bytes=_vmem_limit(48 * 1024 * 1024)),
        name="diff_attn",
    )(q, k, v, band, lam_qk, sub_g)


def _merge_kernel(x_ref, ya_ref, yb_ref, gate_ref, wa_ref, wb_ref, wo_ref, o_ref):
    yb = jnp.concatenate([yb_ref[0, hd] for hd in range(B_HEADS)], axis=-1)
    br_a = jnp.dot(ya_ref[...], wa_ref[...], preferred_element_type=jnp.float32)
    br_b = jnp.dot(yb, wb_ref[...], preferred_element_type=jnp.float32)
    gate = gate_ref[...].astype(jnp.float32)
    merged = (jax.nn.sigmoid(gate[:, :D_MODEL]) * br_a
              + jax.nn.sigmoid(gate[:, D_MODEL:]) * br_b)
    o_ref[...] = x_ref[...] + jnp.dot(merged.astype(jnp.bfloat16), wo_ref[...],
                                      preferred_element_type=jnp.float32)


def _merge_call(x, ya, yb, gates, w_br_a, w_br_b, w_out):
    bsz, s, _ = x.shape
    m = bsz * s
    tm = TM_MERGE
    tiles_per_seq = s // tm
    row = lambda width: pl.BlockSpec((tm, width), lambda i: (i, 0))
    const = lambda shape: pl.BlockSpec(shape, lambda i: (0,) * len(shape),
                                       pipeline_mode=pl.Buffered(1))
    out = pl.pallas_call(
        _merge_kernel,
        out_shape=jax.ShapeDtypeStruct((m, D_MODEL), jnp.float32),
        grid=(m // tm,),
        in_specs=[
            row(D_MODEL), row(A_WIDTH),
            pl.BlockSpec((1, B_HEADS, tm, HEAD_W),
                         lambda i: (i // tiles_per_seq, 0, i % tiles_per_seq, 0)),
            row(2 * D_MODEL),
            const((A_WIDTH, D_MODEL)), const((B_WIDTH, D_MODEL)), const((D_MODEL, D_MODEL)),
        ],
        out_specs=row(D_MODEL),
        compiler_params=pltpu.CompilerParams(
            dimension_semantics=("parallel",),
            vmem_limit_bytes=_vmem_limit(48 * 1024 * 1024)),
        name="merge_out",
    )(x.reshape(m, D_MODEL), ya, yb, gates, w_br_a, w_br_b, w_out)
    return out.reshape(bsz, s, D_MODEL)


def _ffn_kernel(final, x_ref, xp_ref, xn_ref, g_ref, wup_ref, cw_ref, cb_ref, wdn_ref, gf_ref,
                o_ref, h_ref, *slot_refs):
    up_refs, gated_refs = slot_refs[:FFN_SLOTS], slot_refs[FFN_SLOTS:]
    tm = x_ref.shape[1]
    j = pl.program_id(1)
    x = x_ref[0]
    keep_p = (j > 0).astype(jnp.float32)
    keep_n = (j < pl.num_programs(1) - 1).astype(jnp.float32)
    g = g_ref[...]
    h_ref[...] = jnp.concatenate(
        [_rms(xp_ref[0], g) * keep_p, _rms(x, g), _rms(xn_ref[0], g) * keep_n],
        axis=0).astype(h_ref.dtype)
    o_ref[0] = x

    nothing = lambda: ()

    def ff_cols(half, c):
        return slice(half * D_FF + c * FC, half * D_FF + (c + 1) * FC)

    def up_stage(c, slot):
        def piece(half):
            def run():
                up_refs[slot][half] = jnp.dot(h_ref[...], wup_ref[:, ff_cols(half, c)],
                                              preferred_element_type=jnp.float32)
            return run
        return [piece(0), piece(1)], nothing

    def conv(up_ref, half, c, r0):
        n = FFN_RB + 2 * HALO
        ext = up_ref[half, r0:r0 + n, :]
        w = cw_ref[:, ff_cols(half, c)]
        prev = pltpu.roll(ext, 1, 0)
        nxt = pltpu.roll(ext, n - 1, 0)
        mid = slice(HALO, HALO + FFN_RB)
        return (prev[mid] * w[0:1] + ext[mid] * w[1:2] + nxt[mid] * w[2:3]
                + cb_ref[:, ff_cols(half, c)])

    def gate_stage(c, slot, _):
        def piece(r0):
            def run():
                a = conv(up_refs[slot], 0, c, r0)
                half_b = conv(up_refs[slot], 1, c, r0)
                t = jnp.tanh(a * (GELU_C0 + GELU_C1 * (a * a)))
                gated_refs[slot][r0:r0 + FFN_RB, :] = ((a + a * t) * half_b).astype(jnp.bfloat16)
            return run
        return [piece(r0) for r0 in range(0, tm, FFN_RB)], nothing

    def down_stage(c, slot, _):
        def piece(n0):
            def run():
                cols = slice(n0, n0 + V7X_MXU_DIM)
                o_ref[0, :, cols] += jnp.dot(gated_refs[slot][...],
                                             wdn_ref[c * FC:(c + 1) * FC, cols],
                                             preferred_element_type=jnp.float32)
            return run
        return [piece(n0) for n0 in range(0, D_MODEL, V7X_MXU_DIM)], nothing

    _skewed_pipeline(N_FC, up_stage, gate_stage, down_stage, slots=FFN_SLOTS, unroll=True)
    if final:
        o_ref[0] = _rms(o_ref[0], gf_ref[...])


def _ffn_call(x, g_ffn, w_up, conv_w, conv_b, w_down, g_final, final):
    bsz, s, _ = x.shape
    tm = TM_FFN
    nt = s // tm
    hb = tm // HALO
    const = lambda shape: pl.BlockSpec(shape, lambda b, j: (0,) * len(shape),
                                       pipeline_mode=pl.Buffered(1))
    return pl.pallas_call(
        functools.partial(_ffn_kernel, final),
        out_shape=jax.ShapeDtypeStruct(x.shape, jnp.float32),
        grid=(bsz, nt),
        in_specs=[
            pl.BlockSpec((1, tm, D_MODEL), lambda b, j: (b, j, 0)),
            pl.BlockSpec((1, HALO, D_MODEL), lambda b, j: (b, jnp.maximum(j * hb - 1, 0), 0)),
            pl.BlockSpec((1, HALO, D_MODEL),
                         lambda b, j: (b, jnp.minimum((j + 1) * hb, nt * hb - 1), 0)),
            const((1, D_MODEL)),
            const((D_MODEL, 2 * D_FF)),
            const((CONV_W, 2 * D_FF)),
            const((1, 2 * D_FF)),
            const((D_FF, D_MODEL)),
            const((1, D_MODEL)),
        ],
        out_specs=pl.BlockSpec((1, tm, D_MODEL), lambda b, j: (b, j, 0)),
        scratch_shapes=[
            pltpu.VMEM((tm + 2 * HALO, D_MODEL), jnp.bfloat16),
            *[pltpu.VMEM((2, tm + 2 * HALO, FC), jnp.float32)] * FFN_SLOTS,
            *[pltpu.VMEM((tm, FC), jnp.bfloat16)] * FFN_SLOTS,
        ],
        compiler_params=pltpu.CompilerParams(
            dimension_semantics=("parallel", "parallel"),
            vmem_limit_bytes=_vmem_limit(56 * 1024 * 1024)),
        name="conv_ffn",
    )(x, x, x, g_ffn, w_up, conv_w, conv_b, w_down, g_final)


def _prep_layer(l, g_mix, w_in, sgu_g, sgu_w, sgu_b, lam_qk, sub_g, w_br, w_out,
                g_ffn, w_up, conv_w, conv_b, w_down):
    bf = jnp.bfloat16
    q_lo, q_hi = 2 * A_WIDTH, 2 * A_WIDTH + B_WIDTH
    col_scale = jnp.ones((IN_COLS,), jnp.float32).at[q_lo:q_hi].set(ATTN_SCALE * LOG2E)
    gate_scale = jnp.ones((2 * D_FF,), jnp.float32).at[D_FF:].set(0.5)
    return dict(
        g_mix=g_mix[l][None],
        w_in=(w_in[l] * col_scale).astype(bf),
        sgu_g=sgu_g[l][None],
        sgu_w=sgu_w[l].astype(bf),
        sgu_bfull=jnp.repeat(sgu_b[l].T, CHUNK, axis=1),
        lam_qk=lam_qk[l],
        sub_g=sub_g[l][None],
        w_br_a=w_br[l, 0].astype(bf),
        w_br_b=w_br[l, 1].astype(bf),
        w_out=w_out[l].astype(bf),
        g_ffn=g_ffn[l][None],
        w_up=w_up[l].astype(bf),
        conv_w=conv_w[l] * gate_scale,
        conv_b=(conv_b[l] * gate_scale)[None],
        w_down=w_down[l].astype(bf),
    )


def _trunk(x, band, layers, g_final):
    for l, p in enumerate(layers):
        lam_init = 0.8 - 0.6 * math.exp(-0.3 * l)
        ya, q, k, v, gates = _proj_call(x, p["g_mix"], p["w_in"], p["sgu_g"], p["sgu_w"],
                                        p["sgu_bfull"])
        yb = _attn_call(q, k, v, band, p["lam_qk"], p["sub_g"], lam_init)
        x = _merge_call(x, ya, yb, gates, p["w_br_a"], p["w_br_b"], p["w_out"])
        x = _ffn_call(x, p["g_ffn"], p["w_up"], p["conv_w"], p["conv_b"], p["w_down"],
                      g_final, final=(l == len(layers) - 1))
    return x


def kernel(x_prompt, x_sample, rel_bias, g_mix, w_in, sgu_g, sgu_w, sgu_b, lam_qk, sub_g,
           w_br, w_out, g_ffn, w_up, conv_w, conv_b, w_down, g_final):
    band = _bias_band(rel_bias)
    layers = [_prep_layer(l, g_mix, w_in, sgu_g, sgu_w, sgu_b, lam_qk, sub_g, w_br, w_out,
                          g_ffn, w_up, conv_w, conv_b, w_down) for l in range(DEPTH)]
    g_fin = g_final[None]
    return (_trunk(x_prompt, band, layers, g_fin), _trunk(x_sample, band, layers, g_fin))
```

```python
import functools
import math

import jax
import jax.numpy as jnp
from jax import lax
from jax.experimental import pallas as pl
from jax.experimental.pallas import tpu as pltpu

D_MODEL = 1024
DEPTH = 2
CHUNK = 128
A_GROUPS = 8
A_WIDTH = D_MODEL
B_HEADS = 8
B_HEAD_DIM = D_MODEL // (2 * B_HEADS)
HEAD_W = 2 * B_HEAD_DIM
B_WIDTH = B_HEADS * HEAD_W
ATTN_SCALE = B_HEAD_DIM ** -0.5
N_BUCKETS = 32
MAX_DIST = 128
D_FF = 2816
CONV_W = 3
EPS = 1e-6
IN_COLS = 2 * A_WIDTH + 3 * B_WIDTH + 2 * D_MODEL

V7X_LANES = 128
V7X_SUBLANES = 8
V7X_MXU_DIM = 256
V7X_VMEM_BYTES = 64 * 1024 * 1024

TM_PROJ = 512
TM_MERGE = 512
TM_FFN = 512
HALO = V7X_SUBLANES
FC = V7X_MXU_DIM
N_FC = D_FF // FC
FFN_RB = TM_FFN
FFN_SLOTS = 2
TQ = V7X_MXU_DIM
CK = V7X_MXU_DIM
ATTN_RB = 64
ATTN_SLOTS = 3
ATTN_HPS = 2
N_BAND = 5
VT_ROWS = HEAD_W + 2 * V7X_SUBLANES
LOG2E = math.log2(math.e)
GELU_C0 = math.sqrt(2.0 / math.pi)
GELU_C1 = 0.044715 * GELU_C0

assert D_FF % FC == 0
assert MAX_DIST <= CK and MAX_DIST <= TQ


def _vmem_limit(nbytes):
    return int(min(nbytes, V7X_VMEM_BYTES - 4 * 1024 * 1024))


def _rms(x, g):
    return x * lax.rsqrt(jnp.mean(x * x, axis=-1, keepdims=True) + EPS) * g


def _run_interleaved(*stages):
    order = sorted(((i + 0.5) / len(pieces), si, i)
                   for si, (pieces, _) in enumerate(stages) for i in range(len(pieces)))
    for _, si, i in order:
        stages[si][0][i]()
    return [result() for _, result in stages]


def _skewed_pipeline(n, stage_a, stage_b, stage_c, slots=2, unroll=False):
    assert n >= 2 and slots >= 2
    run = _run_interleaved

    def step(t, t_static, ra, rb):
        ra_new, rb_new, _ = run(stage_a(t, t_static % slots),
                                stage_b(t - 1, (t_static - 1) % slots, ra),
                                stage_c(t - 2, (t_static - 2) % slots, rb))
        return ra_new, rb_new

    ra0, = run(stage_a(0, 0))
    ra, rb = run(stage_a(1, 1), stage_b(0, 0, ra0))
    t0 = 2
    while (n - t0) % slots:
        ra, rb = step(t0, t0, ra, rb)
        t0 += 1

    def body(k, carry):
        for j in range(slots):
            carry = step(t0 + slots * k + j, t0 + j, *carry)
        return carry

    if unroll:
        for k in range((n - t0) // slots):
            ra, rb = body(k, (ra, rb))
    else:
        ra, rb = lax.fori_loop(0, (n - t0) // slots, body, (ra, rb))
    rb_last, _ = run(stage_b(n - 1, (n - 1) % slots, ra), stage_c(n - 2, (n - 2) % slots, rb))
    run(stage_c(n - 1, (n - 1) % slots, rb_last))


def _rel_bucket(rel):
    nb = N_BUCKETS // 2
    max_exact = nb // 2
    ret = jnp.where(rel > 0, nb, 0)
    n = jnp.abs(rel)
    nf = jnp.maximum(n, 1).astype(jnp.float32)
    large = max_exact + (jnp.log(nf / max_exact) / math.log(MAX_DIST / max_exact)
                         * (nb - max_exact)).astype(jnp.int32)
    large = jnp.minimum(large, nb - 1)
    return ret + jnp.where(n < max_exact, n, large)


def _band_kernel(table_ref, idx_ref, out_ref):
    h = pl.program_id(0)
    idx = idx_ref[...]
    acc = jnp.zeros(idx.shape, jnp.float32)
    for b in range(N_BUCKETS):
        acc = jnp.where(idx == b, table_ref[b, h] * LOG2E, acc)
    out_ref[0] = acc


def _bias_band(rel_bias):
    d = jnp.arange(N_BAND, dtype=jnp.int32)[:, None, None] - N_BAND // 2
    kk = jnp.arange(CK, dtype=jnp.int32)[None, :, None]
    qq = jnp.arange(TQ, dtype=jnp.int32)[None, None, :]
    idx = _rel_bucket(d * CK + kk - qq)
    return pl.pallas_call(
        _band_kernel,
        out_shape=jax.ShapeDtypeStruct((B_HEADS, N_BAND, CK, TQ), jnp.float32),
        grid=(B_HEADS,),
        in_specs=[
            pl.BlockSpec(memory_space=pltpu.SMEM),
            pl.BlockSpec((N_BAND, CK, TQ), lambda h: (0, 0, 0)),
        ],
        out_specs=pl.BlockSpec((1, N_BAND, CK, TQ), lambda h: (h, 0, 0, 0)),
        name="bias_band",
    )(rel_bias, idx)


def _proj_kernel(x_ref, g_ref, w_ref, sg_ref, ws_ref, bs_ref,
                 ya_ref, q_ref, k_ref, v_ref, gate_ref):
    tm = x_ref.shape[0]
    h = _rms(x_ref[...], g_ref[...]).astype(jnp.bfloat16)

    def proj(lo, hi):
        return jnp.dot(h, w_ref[:, lo:hi], preferred_element_type=jnp.float32)

    u = proj(0, A_WIDTH)
    vn = _rms(proj(A_WIDTH, 2 * A_WIDTH), sg_ref[...]).astype(jnp.bfloat16)
    for n in range(tm // CHUNK):
        rows = slice(n * CHUNK, (n + 1) * CHUNK)
        for g in range(A_GROUPS):
            cols = slice(g * CHUNK, (g + 1) * CHUNK)
            mixed = jnp.dot(ws_ref[g], vn[rows, cols], preferred_element_type=jnp.float32)
            ya_ref[rows, cols] = (u[rows, cols] * (mixed + bs_ref[:, cols])).astype(ya_ref.dtype)

    base = 2 * A_WIDTH
    for i, o_ref in enumerate((q_ref, k_ref, v_ref)):
        r = proj(base + i * B_WIDTH, base + (i + 1) * B_WIDTH)
        for hd in range(B_HEADS):
            o_ref[0, hd] = r[:, hd * HEAD_W:(hd + 1) * HEAD_W].astype(o_ref.dtype)
    base += 3 * B_WIDTH
    gate_ref[...] = proj(base, base + 2 * D_MODEL).astype(gate_ref.dtype)


def _proj_call(x, g_mix, w_in, sgu_g, sgu_w, sgu_bfull):
    bsz, s, _ = x.shape
    m = bsz * s
    tm = TM_PROJ
    tiles_per_seq = s // tm
    const = lambda shape: pl.BlockSpec(shape, lambda i: (0,) * len(shape),
                                       pipeline_mode=pl.Buffered(1))
    head_spec = pl.BlockSpec((1, B_HEADS, tm, HEAD_W),
                             lambda i: (i // tiles_per_seq, 0, i % tiles_per_seq, 0))
    head_shape = jax.ShapeDtypeStruct((bsz, B_HEADS, s, HEAD_W), jnp.bfloat16)
    return pl.pallas_call(
        _proj_kernel,
        out_shape=(jax.ShapeDtypeStruct((m, A_WIDTH), jnp.bfloat16),
                   head_shape, head_shape, head_shape,
                   jax.ShapeDtypeStruct((m, 2 * D_MODEL), jnp.bfloat16)),
        grid=(m // tm,),
        in_specs=[
            pl.BlockSpec((tm, D_MODEL), lambda i: (i, 0)),
            const((1, D_MODEL)),
            const((D_MODEL, IN_COLS)),
            const((1, A_WIDTH)),
            const((A_GROUPS, CHUNK, CHUNK)),
            const((CHUNK, A_WIDTH)),
        ],
        out_specs=(pl.BlockSpec((tm, A_WIDTH), lambda i: (i, 0)),
                   head_spec, head_spec, head_spec,
                   pl.BlockSpec((tm, 2 * D_MODEL), lambda i: (i, 0))),
        compiler_params=pltpu.CompilerParams(
            dimension_semantics=("parallel",),
            vmem_limit_bytes=_vmem_limit(56 * 1024 * 1024)),
        name="proj_sgu",
    )(x.reshape(m, D_MODEL), g_mix, w_in, sgu_g, sgu_w, sgu_bfull)


def _attn_kernel(lam_init, q_ref, k_ref, v_ref, band_ref, lq_ref, subg_ref, o_ref,
                 vt_ref, *slot_refs):
    l_refs, p_refs = slot_refs[:ATTN_SLOTS], slot_refs[ATTN_SLOTS:]
    s = q_ref.shape[2]
    nq, nk = s // TQ, s // CK
    rg = ATTN_RB // V7X_SUBLANES

    lq = lq_ref[...]
    lam = (jnp.exp(jnp.sum(lq[0:1] * lq[1:2], axis=-1, keepdims=True))
           - jnp.exp(jnp.sum(lq[2:3] * lq[3:4], axis=-1, keepdims=True)) + lam_init)

    pad_row = lax.broadcasted_iota(jnp.int32, (VT_ROWS - HEAD_W, s), 0)
    for hd in range(ATTN_HPS):
        for c in range(nk):
            cols = slice(c * CK, (c + 1) * CK)
            vt_ref[hd, :HEAD_W, cols] = (
                v_ref[0, hd, cols, :].astype(jnp.float32).T.astype(vt_ref.dtype))
        vt_ref[hd, HEAD_W:, :] = jnp.where(pad_row == 0, 1.0, 0.0).astype(vt_ref.dtype)

    def split(i):
        return divmod(i, nq) if isinstance(i, int) else (i // nq, i % nq)

    lane = lax.broadcasted_iota(jnp.int32, (TQ, HEAD_W), 1)
    nt_dims = (((1,), (1,)), ((), ()))

    def q_rows(qi):
        return pl.ds(qi * TQ, TQ) if isinstance(qi, int) else pl.ds(pl.multiple_of(qi * TQ, TQ), TQ)

    def band_index(c, qi):
        lo, hi = -(N_BAND // 2), N_BAND // 2
        if isinstance(qi, int):
            return min(max(c - qi, lo), hi) - lo
        return jnp.clip(c - qi, lo, hi) - lo

    def logit_stage(i, slot):
        hd, qi = split(i)
        l_ref = l_refs[slot]
        qm = []
        mx = [None, None]

        def piece(m, c):
            def run():
                if not qm:
                    q = q_ref[0, hd, q_rows(qi), :].astype(jnp.float32)
                    qm.append(jnp.where(lane < B_HEAD_DIM, q, 0.0).astype(jnp.bfloat16))
                    qm.append(jnp.where(lane >= B_HEAD_DIM, q, 0.0).astype(jnp.bfloat16))
                logit = lax.dot_general(k_ref[0, hd, c * CK:(c + 1) * CK, :], qm[m], nt_dims,
                                        preferred_element_type=jnp.float32)
                d = band_index(c, qi)
                for r in range(0, CK, ATTN_RB):
                    rows = slice(r, r + ATTN_RB)
                    lc = logit[rows] + band_ref[hd, d, rows, :]
                    l_ref[m, c, rows, :] = lc
                    cm = jnp.max(lc.reshape(rg, V7X_SUBLANES, TQ), axis=0)
                    mx[m] = cm if mx[m] is None else jnp.maximum(mx[m], cm)
            return run

        pieces = [piece(m, c) for c in range(nk) for m in range(2)]
        return pieces, lambda: tuple(jnp.max(v, axis=0, keepdims=True) for v in mx)

    def exp_stage(qi, slot, mrow):
        l_ref, p_ref = l_refs[slot], p_refs[slot]

        def piece(m, r):
            def run():
                x = l_ref[m, r // CK, r % CK:r % CK + ATTN_RB, :] - mrow[m]
                p_ref[m, r:r + ATTN_RB, :] = jnp.exp2(x.astype(p_ref.dtype))
            return run

        return [piece(m, r) for m in range(2) for r in range(0, s, ATTN_RB)], lambda: ()

    def value_stage(i, slot, _):
        hd, qi = split(i)
        acc = []

        def matmul(m):
            def run():
                acc.append(jnp.dot(vt_ref[hd], p_refs[slot][m],
                                   preferred_element_type=jnp.float32))
            return run

        def finish():
            sums = [a[HEAD_W:HEAD_W + 1] for a in acc]
            o = acc[0][:HEAD_W] / sums[0] - acc[1][:HEAD_W] * (lam / sums[1])
            o = o * lax.rsqrt(jnp.mean(o * o, axis=0, keepdims=True) + EPS)
            o = o.T * subg_ref[...] * (1.0 - lam_init)
            o_ref[0, hd, q_rows(qi), :] = o.astype(o_ref.dtype)

        return [matmul(0), matmul(1), finish], lambda: ()

    _skewed_pipeline(ATTN_HPS * nq, logit_stage, exp_stage, value_stage, slots=ATTN_SLOTS)


def _attn_call(q, k, v, band, lam_qk, sub_g, lam_init):
    bsz, _, s, _ = q.shape
    head = pl.BlockSpec((1, ATTN_HPS, s, HEAD_W), lambda b, h: (b, h, 0, 0))
    return pl.pallas_call(
        functools.partial(_attn_kernel, lam_init),
        out_shape=jax.ShapeDtypeStruct(q.shape, jnp.bfloat16),
        grid=(bsz, B_HEADS // ATTN_HPS),
        in_specs=[
            head, head, head,
            pl.BlockSpec((ATTN_HPS, N_BAND, CK, TQ), lambda b, h: (h, 0, 0, 0)),
            pl.BlockSpec((4, B_HEAD_DIM), lambda b, h: (0, 0)),
            pl.BlockSpec((1, HEAD_W), lambda b, h: (0, 0)),
        ],
        out_specs=head,
        scratch_shapes=[
            pltpu.VMEM((ATTN_HPS, VT_ROWS, s), jnp.bfloat16),
            *[pltpu.VMEM((2, s // CK, CK, TQ), jnp.float32)] * ATTN_SLOTS,
            *[pltpu.VMEM((2, s, TQ), jnp.bfloat16)] * ATTN_SLOTS,
        ],
        compiler_params=pltpu.CompilerParams(
            dimension_semantics=("parallel", "parallel"),
            vmem_limit_bytes=_vmem_limit(48 * 1024 * 1024)),
        name="diff_attn",
    )(q, k, v, band, lam_qk, sub_g)


def _merge_kernel(x_ref, ya_ref, yb_ref, gate_ref, wa_ref, wb_ref, wo_ref, o_ref):
    yb = jnp.concatenate([yb_ref[0, hd] for hd in range(B_HEADS)], axis=-1)
    br_a = jnp.dot(ya_ref[...], wa_ref[...], preferred_element_type=jnp.float32)
    br_b = jnp.dot(yb, wb_ref[...], preferred_element_type=jnp.float32)
    gate = gate_ref[...].astype(jnp.float32)
    merged = (jax.nn.sigmoid(gate[:, :D_MODEL]) * br_a
              + jax.nn.sigmoid(gate[:, D_MODEL:]) * br_b)
    o_ref[...] = x_ref[...] + jnp.dot(merged.astype(jnp.bfloat16), wo_ref[...],
                                      preferred_element_type=jnp.float32)


def _merge_call(x, ya, yb, gates, w_br_a, w_br_b, w_out):
    bsz, s, _ = x.shape
    m = bsz * s
    tm = TM_MERGE
    tiles_per_seq = s // tm
    row = lambda width: pl.BlockSpec((tm, width), lambda i: (i, 0))
    const = lambda shape: pl.BlockSpec(shape, lambda i: (0,) * len(shape),
                                       pipeline_mode=pl.Buffered(1))
    out = pl.pallas_call(
        _merge_kernel,
        out_shape=jax.ShapeDtypeStruct((m, D_MODEL), jnp.float32),
        grid=(m // tm,),
        in_specs=[
            row(D_MODEL), row(A_WIDTH),
            pl.BlockSpec((1, B_HEADS, tm, HEAD_W),
                         lambda i: (i // tiles_per_seq, 0, i % tiles_per_seq, 0)),
            row(2 * D_MODEL),
            const((A_WIDTH, D_MODEL)), const((B_WIDTH, D_MODEL)), const((D_MODEL, D_MODEL)),
        ],
        out_specs=row(D_MODEL),
        compiler_params=pltpu.CompilerParams(
            dimension_semantics=("parallel",),
            vmem_limit_bytes=_vmem_limit(48 * 1024 * 1024)),
        name="merge_out",
    )(x.reshape(m, D_MODEL), ya, yb, gates, w_br_a, w_br_b, w_out)
    return out.reshape(bsz, s, D_MODEL)


def _ffn_kernel(final, x_ref, xp_ref, xn_ref, g_ref, wup_ref, cw_ref, cb_ref, wdn_ref, gf_ref,
                o_ref, h_ref, *slot_refs):
    up_refs, gated_refs = slot_refs[:FFN_SLOTS], slot_refs[FFN_SLOTS:]
    tm = x_ref.shape[1]
    j = pl.program_id(1)
    x = x_ref[0]
    keep_p = (j > 0).astype(jnp.float32)
    keep_n = (j < pl.num_programs(1) - 1).astype(jnp.float32)
    g = g_ref[...]
    h_ref[...] = jnp.concatenate(
        [_rms(xp_ref[0], g) * keep_p, _rms(x, g), _rms(xn_ref[0], g) * keep_n],
        axis=0).astype(h_ref.dtype)
    o_ref[0] = x

    nothing = lambda: ()

    def ff_cols(half, c):
        return slice(half * D_FF + c * FC, half * D_FF + (c + 1) * FC)

    def up_stage(c, slot):
        def piece(half):
            def run():
                up_refs[slot][half] = jnp.dot(h_ref[...], wup_ref[:, ff_cols(half, c)],
                                              preferred_element_type=jnp.float32)
            return run
        return [piece(0), piece(1)], nothing

    def conv(up_ref, half, c, r0):
        n = FFN_RB + 2 * HALO
        ext = up_ref[half, r0:r0 + n, :]
        w = cw_ref[:, ff_cols(half, c)]
        prev = pltpu.roll(ext, 1, 0)
        nxt = pltpu.roll(ext, n - 1, 0)
        mid = slice(HALO, HALO + FFN_RB)
        return (prev[mid] * w[0:1] + ext[mid] * w[1:2] + nxt[mid] * w[2:3]
                + cb_ref[:, ff_cols(half, c)])

    def gate_stage(c, slot, _):
        def piece(r0):
            def run():
                a = conv(up_refs[slot], 0, c, r0)
                half_b = conv(up_refs[slot], 1, c, r0)
                t = jnp.tanh(a * (GELU_C0 + GELU_C1 * (a * a)))
                gated_refs[slot][r0:r0 + FFN_RB, :] = ((a + a * t) * half_b).astype(jnp.bfloat16)
            return run
        return [piece(r0) for r0 in range(0, tm, FFN_RB)], nothing

    def down_stage(c, slot, _):
        def piece(n0):
            def run():
                cols = slice(n0, n0 + V7X_MXU_DIM)
                o_ref[0, :, cols] += jnp.dot(gated_refs[slot][...],
                                             wdn_ref[c * FC:(c + 1) * FC, cols],
                                             preferred_element_type=jnp.float32)
            return run
        return [piece(n0) for n0 in range(0, D_MODEL, V7X_MXU_DIM)], nothing

    _skewed_pipeline(N_FC, up_stage, gate_stage, down_stage, slots=FFN_SLOTS, unroll=True)
    if final:
        o_ref[0] = _rms(o_ref[0], gf_ref[...])


def _ffn_call(x, g_ffn, w_up, conv_w, conv_b, w_down, g_final, final):
    bsz, s, _ = x.shape
    tm = TM_FFN
    nt = s // tm
    hb = tm // HALO
    const = lambda shape: pl.BlockSpec(shape, lambda b, j: (0,) * len(shape),
                                       pipeline_mode=pl.Buffered(1))
    return pl.pallas_call(
        functools.partial(_ffn_kernel, final),
        out_shape=jax.ShapeDtypeStruct(x.shape, jnp.float32),
        grid=(bsz, nt),
        in_specs=[
            pl.BlockSpec((1, tm, D_MODEL), lambda b, j: (b, j, 0)),
            pl.BlockSpec((1, HALO, D_MODEL), lambda b, j: (b, jnp.maximum(j * hb - 1, 0), 0)),
            pl.BlockSpec((1, HALO, D_MODEL),
                         lambda b, j: (b, jnp.minimum((j + 1) * hb, nt * hb - 1), 0)),
            const((1, D_MODEL)),
            const((D_MODEL, 2 * D_FF)),
            const((CONV_W, 2 * D_FF)),
            const((1, 2 * D_FF)),
            const((D_FF, D_MODEL)),
            const((1, D_MODEL)),
        ],
        out_specs=pl.BlockSpec((1, tm, D_MODEL), lambda b, j: (b, j, 0)),
        scratch_shapes=[
            pltpu.VMEM((tm + 2 * HALO, D_MODEL), jnp.bfloat16),
            *[pltpu.VMEM((2, tm + 2 * HALO, FC), jnp.float32)] * FFN_SLOTS,
            *[pltpu.VMEM((tm, FC), jnp.bfloat16)] * FFN_SLOTS,
        ],
        compiler_params=pltpu.CompilerParams(
            dimension_semantics=("parallel", "parallel"),
            vmem_limit_bytes=_vmem_limit(56 * 1024 * 1024)),
        name="conv_ffn",
    )(x, x, x, g_ffn, w_up, conv_w, conv_b, w_down, g_final)


def _prep_layer(l, g_mix, w_in, sgu_g, sgu_w, sgu_b, lam_qk, sub_g, w_br, w_out,
                g_ffn, w_up, conv_w, conv_b, w_down):
    bf = jnp.bfloat16
    q_lo, q_hi = 2 * A_WIDTH, 2 * A_WIDTH + B_WIDTH
    col_scale = jnp.ones((IN_COLS,), jnp.float32).at[q_lo:q_hi].set(ATTN_SCALE * LOG2E)
    gate_scale = jnp.ones((2 * D_FF,), jnp.float32).at[D_FF:].set(0.5)
    return dict(
        g_mix=g_mix[l][None],
        w_in=(w_in[l] * col_scale).astype(bf),
        sgu_g=sgu_g[l][None],
        sgu_w=sgu_w[l].astype(bf),
        sgu_bfull=jnp.repeat(sgu_b[l].T, CHUNK, axis=1),
        lam_qk=lam_qk[l],
        sub_g=sub_g[l][None],
        w_br_a=w_br[l, 0].astype(bf),
        w_br_b=w_br[l, 1].astype(bf),
        w_out=w_out[l].astype(bf),
        g_ffn=g_ffn[l][None],
        w_up=w_up[l].astype(bf),
        conv_w=conv_w[l] * gate_scale,
        conv_b=(conv_b[l] * gate_scale)[None],
        w_down=w_down[l].astype(bf),
    )


def _trunk(x, band, layers, g_final):
    for l, p in enumerate(layers):
        lam_init = 0.8 - 0.6 * math.exp(-0.3 * l)
        ya, q, k, v, gates = _proj_call(x, p["g_mix"], p["w_in"], p["sgu_g"], p["sgu_w"],
                                        p["sgu_bfull"])
        yb = _attn_call(q, k, v, band, p["lam_qk"], p["sub_g"], lam_init)
        x = _merge_call(x, ya, yb, gates, p["w_br_a"], p["w_br_b"], p["w_out"])
        x = _ffn_call(x, p["g_ffn"], p["w_up"], p["conv_w"], p["conv_b"], p["w_down"],
                      g_final, final=(l == len(layers) - 1))
    return x


def kernel(x_prompt, x_sample, rel_bias, g_mix, w_in, sgu_g, sgu_w, sgu_b, lam_qk, sub_g,
           w_br, w_out, g_ffn, w_up, conv_w, conv_b, w_down, g_final):
    band = _bias_band(rel_bias)
    layers = [_prep_layer(l, g_mix, w_in, sgu_g, sgu_w, sgu_b, lam_qk, sub_g, w_br, w_out,
                          g_ffn, w_up, conv_w, conv_b, w_down) for l in range(DEPTH)]
    g_fin = g_final[None]
    return (_trunk(x_prompt, band, layers, g_fin), _trunk(x_sample, band, layers, g_fin))
```
